```python
import jax, jax.numpy as jnp
from jax import lax
import numpy as np

D_MODEL = 1024
BATCH = 8
SEQ = 4096
DEPTH = 1

SB_HEADS = 16
SB_HEAD_DIM = 64
SB_WIDTH = SB_HEADS * SB_HEAD_DIM
SB_BLOCK = 128
GLA_HEADS = 4
GLA_KEY_WIDTH = D_MODEL // 2
GLA_VALUE_WIDTH = D_MODEL
GLA_DK = GLA_KEY_WIDTH // GLA_HEADS
GLA_DV = GLA_VALUE_WIDTH // GLA_HEADS
GLA_GATE_RANK = 16
GLA_GATE_TAU = 16.0
GLA_CHUNK = 64
D_FF = -(-8 * D_MODEL // (3 * 256)) * 256
RMS_EPS = 1e-6
IN_SIZES = (SB_WIDTH, SB_WIDTH, SB_WIDTH,
            GLA_KEY_WIDTH, GLA_KEY_WIDTH, GLA_VALUE_WIDTH, GLA_VALUE_WIDTH, GLA_GATE_RANK,
            D_MODEL, D_MODEL)
IN_WIDTH = sum(IN_SIZES)

kernel_name = "hybrid_stickbreak_gla_gated_block"


def rms_norm(x, g):
    xf = x.astype(jnp.float32)
    y = xf * lax.rsqrt(jnp.mean(xf * xf, axis=-1, keepdims=True) + RMS_EPS)
    return (y * g.astype(jnp.float32)).astype(x.dtype)


def stick_breaking_attention(q, k, v):
    b, h, s, dh = q.shape
    nb = s // SB_BLOCK
    scale = dh ** -0.5
    kf = k.astype(jnp.float32)
    vf = v.astype(jnp.float32)
    key_pos = jnp.arange(s)

    def block(i):
        start = i * SB_BLOCK
        qb = lax.dynamic_slice_in_dim(q, start, SB_BLOCK, axis=2).astype(jnp.float32)
        z = jnp.einsum('bhtd,bhsd->bhts', qb, kf) * scale
        q_pos = start + jnp.arange(SB_BLOCK)
        mask = key_pos[None, :] < q_pos[:, None]
        log_not = jnp.where(mask, jax.nn.log_sigmoid(-z), 0.0)
        after = lax.cumsum(log_not, axis=3, reverse=True) - log_not
        w = jnp.where(mask, jnp.exp(jax.nn.log_sigmoid(z) + after), 0.0)
        return jnp.einsum('bhts,bhsd->bhtd', w, vf)

    o = lax.map(block, jnp.arange(nb))
    o = jnp.transpose(o, (1, 0, 3, 2, 4)).reshape(b, s, h * dh)
    return o.astype(v.dtype)


def gla_chunked(q, k, v, log_a):
    b, h, s, dk = q.shape
    dv = v.shape[-1]
    nc = s // GLA_CHUNK

    def to_chunks(t):
        t = t.astype(jnp.float32)
        return jnp.moveaxis(t.reshape(b, h, nc, GLA_CHUNK, t.shape[-1]), 2, 0)

    qc = to_chunks(q * (dk ** -0.5))
    kc = to_chunks(k)
    vc = to_chunks(v)
    gc = to_chunks(log_a)
    causal = jnp.tril(jnp.ones((GLA_CHUNK, GLA_CHUNK), dtype=bool))

    def step(state, inp):
        qi, ki, vi, gi = inp
        cum = jnp.cumsum(gi, axis=2)
        o_inter = jnp.einsum('bhck,bhkv->bhcv', qi * jnp.exp(cum), state)
        diff = cum[:, :, :, None, :] - cum[:, :, None, :, :]
        decay = jnp.where(causal[:, :, None], jnp.exp(jnp.minimum(diff, 0.0)), 0.0)
        scores = jnp.einsum('bhik,bhjk,bhijk->bhij', qi, ki, decay)
        o_intra = jnp.einsum('bhij,bhjv->bhiv', scores, vi)
        last = cum[:, :, -1:, :]
        state = (state * jnp.exp(last[:, :, 0, :])[..., None]
                 + jnp.einsum('bhck,bhcv->bhkv', ki * jnp.exp(last - cum), vi))
        return state, o_inter + o_intra

    state0 = jnp.zeros((b, h, dk, dv), jnp.float32)
    _, o = lax.scan(step, state0, (qc, kc, vc, gc))
    o = jnp.moveaxis(o, 0, 2).reshape(b, h, s, dv)
    return jnp.transpose(o, (0, 2, 1, 3)).astype(v.dtype)


def setup_inputs(seed: int = 0) -> dict:
    key = jax.random.key(seed)
    ks = jax.random.split(key, 16)
    f32 = jnp.float32
    nrm = lambda k, shape, fan_in: jax.random.normal(k, shape, f32) * (fan_in ** -0.5)
    gain = lambda k, shape: 1.0 + 0.02 * jax.random.normal(k, shape, f32)
    return {
        "x": jax.random.normal(ks[0], (BATCH, SEQ, D_MODEL), f32),
        "norm1_g": gain(ks[1], (DEPTH, D_MODEL)),
        "w_in": nrm(ks[2], (DEPTH, D_MODEL, IN_WIDTH), D_MODEL),
        "sb_q_norm_g": gain(ks[3], (DEPTH, SB_HEAD_DIM)),
        "sb_k_norm_g": gain(ks[4], (DEPTH, SB_HEAD_DIM)),
        "gla_gate_w2": nrm(ks[5], (DEPTH, GLA_GATE_RANK, GLA_KEY_WIDTH), GLA_GATE_RANK),
        "gla_gate_b": 0.1 * jax.random.normal(ks[6], (DEPTH, GLA_KEY_WIDTH), f32),
        "gla_out_norm_g": gain(ks[7], (DEPTH, GLA_DV)),
        "w_branch_sb": nrm(ks[8], (DEPTH, SB_WIDTH, D_MODEL), SB_WIDTH),
        "w_branch_gla": nrm(ks[9], (DEPTH, GLA_VALUE_WIDTH, D_MODEL), GLA_VALUE_WIDTH),
        "w_out": nrm(ks[10], (DEPTH, D_MODEL, D_MODEL), D_MODEL),
        "norm2_g": gain(ks[11], (DEPTH, D_MODEL)),
        "w_ffn_gate": nrm(ks[12], (DEPTH, D_MODEL, D_FF), D_MODEL),
        "w_ffn_up": nrm(ks[13], (DEPTH, D_MODEL, D_FF), D_MODEL),
        "w_ffn_down": nrm(ks[14], (DEPTH, D_FF, D_MODEL), D_FF),
    }


def reference(x, norm1_g, w_in, sb_q_norm_g, sb_k_norm_g, gla_gate_w2, gla_gate_b,
              gla_out_norm_g, w_branch_sb, w_branch_gla, w_out, norm2_g,
              w_ffn_gate, w_ffn_up, w_ffn_down):
    b, s, _ = x.shape
    split_points = [int(v) for v in np.cumsum(IN_SIZES)[:-1]]
    for l in range(DEPTH):
        h = rms_norm(x, norm1_g[l])
        proj = h @ w_in[l]
        (sb_q, sb_k, sb_v, gla_q, gla_k, gla_v, gla_r, gla_lr,
         gate_sb, gate_gla) = jnp.split(proj, split_points, axis=-1)

        heads_sb = lambda t: jnp.transpose(t.reshape(b, s, SB_HEADS, SB_HEAD_DIM), (0, 2, 1, 3))
        q_a = heads_sb(rms_norm(sb_q.reshape(b, s, SB_HEADS, SB_HEAD_DIM), sb_q_norm_g[l]).reshape(b, s, SB_WIDTH))
        k_a = heads_sb(rms_norm(sb_k.reshape(b, s, SB_HEADS, SB_HEAD_DIM), sb_k_norm_g[l]).reshape(b, s, SB_WIDTH))
        o_sb = stick_breaking_attention(q_a, k_a, heads_sb(sb_v))

        gate_logit = (gla_lr @ gla_gate_w2[l] + gla_gate_b[l]).astype(jnp.float32)
        log_a = jax.nn.log_sigmoid(gate_logit) / GLA_GATE_TAU
        heads_k = lambda t: jnp.transpose(t.reshape(b, s, GLA_HEADS, GLA_DK), (0, 2, 1, 3))
        v_b = jnp.transpose(gla_v.reshape(b, s, GLA_HEADS, GLA_DV), (0, 2, 1, 3))
        o_b = gla_chunked(heads_k(gla_q), heads_k(gla_k), v_b, heads_k(log_a))
        o_gla = rms_norm(o_b, gla_out_norm_g[l]).reshape(b, s, GLA_VALUE_WIDTH) * jax.nn.silu(gla_r)

        merged = (jax.nn.sigmoid(gate_sb) * (o_sb @ w_branch_sb[l])
                  + jax.nn.sigmoid(gate_gla) * (o_gla @ w_branch_gla[l]))
        x = x + merged @ w_out[l]

        h2 = rms_norm(x, norm2_g[l])
        x = x + (jax.nn.silu(h2 @ w_ffn_gate[l]) * (h2 @ w_ffn_up[l])) @ w_ffn_down[l]
    return x
```

```python
import functools

import jax
import jax.numpy as jnp
import numpy as np
from jax import lax
from jax.experimental import pallas as pl
from jax.experimental.pallas import tpu as pltpu

F32 = jnp.float32
BF16 = jnp.bfloat16

D_MODEL = 1024
SB_HEADS = 16
SB_HEAD_DIM = 64
SB_WIDTH = SB_HEADS * SB_HEAD_DIM
GLA_HEADS = 4
GLA_KEY_WIDTH = D_MODEL // 2
GLA_VALUE_WIDTH = D_MODEL
GLA_DK = GLA_KEY_WIDTH // GLA_HEADS
GLA_DV = GLA_VALUE_WIDTH // GLA_HEADS
GLA_GATE_RANK = 16
GLA_GATE_TAU = 16.0
D_FF = -(-8 * D_MODEL // (3 * 256)) * 256
RMS_EPS = 1e-6

LANES = 128
MXU_DIM = 256
VMEM_LIMIT_BYTES = 56 * 1024 * 1024

ROW_TILE = 256
SB_TQ = 256
SB_TK = 256
GLA_CHUNK = 64
GLA_STEP = 512
GLA_LEVELS = (32, 16, 8, 4, 2, 1)


def _const_spec(shape):
    nd = len(shape)
    return pl.BlockSpec(shape, lambda *_: (0,) * nd, pipeline_mode=pl.Buffered(1))


def _log_sigmoid(x):
    return jnp.minimum(x, 0.0) - jnp.log(1.0 + jnp.exp(-jnp.abs(x)))


def _inproj_kernel(x_ref, g1_ref, wm_ref, wlr_ref, w2_ref, gb_ref, qg_ref, kg_ref, avg_ref,
                   q_ref, k_ref, v_ref, gq_ref, gk_ref, gv_ref, r_ref, la_ref, gs_ref, gg_ref):
    x = x_ref[...]
    ms = jnp.mean(x * x, axis=-1, keepdims=True)
    h = (x * lax.rsqrt(ms + RMS_EPS) * g1_ref[...]).astype(BF16)

    def proj(c0, cw):
        return jnp.dot(h, wm_ref[:, c0:c0 + cw], preferred_element_type=F32)

    def head_norm(y, gain):
        m = jnp.dot((y * y).astype(BF16), avg_ref[...], preferred_element_type=F32)
        return y * lax.rsqrt(m + RMS_EPS) * gain

    w = MXU_DIM
    for b in range(SB_WIDTH // w):
        q_ref[:, b * w:(b + 1) * w] = head_norm(proj(b * w, w), qg_ref[...]).astype(BF16)
    off = SB_WIDTH
    for b in range(SB_WIDTH // w):
        k_ref[:, b * w:(b + 1) * w] = head_norm(proj(off + b * w, w), kg_ref[...]).astype(BF16)
    off += SB_WIDTH
    cw = 512
    for b in range(SB_WIDTH // cw):
        v_ref[:, b * cw:(b + 1) * cw] = proj(off + b * cw, cw).astype(BF16)
    off += SB_WIDTH
    gq_ref[...] = (proj(off, GLA_KEY_WIDTH) * (GLA_DK ** -0.5)).astype(BF16)
    off += GLA_KEY_WIDTH
    gk_ref[...] = proj(off, GLA_KEY_WIDTH).astype(BF16)
    off += GLA_KEY_WIDTH
    for b in range(GLA_VALUE_WIDTH // cw):
        gv_ref[:, b * cw:(b + 1) * cw] = proj(off + b * cw, cw).astype(BF16)
    off += GLA_VALUE_WIDTH
    for b in range(GLA_VALUE_WIDTH // cw):
        r = proj(off + b * cw, cw)
        r_ref[:, b * cw:(b + 1) * cw] = (r * jax.nn.sigmoid(r)).astype(BF16)
    off += GLA_VALUE_WIDTH
    for b in range(D_MODEL // cw):
        gs_ref[:, b * cw:(b + 1) * cw] = jax.nn.sigmoid(proj(off + b * cw, cw)).astype(BF16)
    off += D_MODEL
    for b in range(D_MODEL // cw):
        gg_ref[:, b * cw:(b + 1) * cw] = jax.nn.sigmoid(proj(off + b * cw, cw)).astype(BF16)

    lr = jnp.dot(h, wlr_ref[...], preferred_element_type=F32)
    logit = jnp.dot(lr.astype(BF16), w2_ref[...], preferred_element_type=F32) + gb_ref[...]
    la_ref[...] = _log_sigmoid(logit) * (1.0 / GLA_GATE_TAU)


def _in_projection(x2, g1, wm, wlr, w2p, gb, qg, kg, avg):
    t = x2.shape[0]
    tm = ROW_TILE
    row = lambda width: pl.BlockSpec((tm, width), lambda i: (i, 0))
    bf = lambda width: jax.ShapeDtypeStruct((t, width), BF16)
    out_shape = (bf(SB_WIDTH), bf(SB_WIDTH), bf(SB_WIDTH),
                 bf(GLA_KEY_WIDTH), bf(GLA_KEY_WIDTH), bf(GLA_VALUE_WIDTH), bf(GLA_VALUE_WIDTH),
                 jax.ShapeDtypeStruct((t, GLA_KEY_WIDTH), F32), bf(D_MODEL), bf(D_MODEL))
    out_specs = (row(SB_WIDTH), row(SB_WIDTH), row(SB_WIDTH),
                 row(GLA_KEY_WIDTH), row(GLA_KEY_WIDTH), row(GLA_VALUE_WIDTH), row(GLA_VALUE_WIDTH),
                 row(GLA_KEY_WIDTH), row(D_MODEL), row(D_MODEL))
    return pl.pallas_call(
        _inproj_kernel,
        grid=(t // tm,),
        in_specs=[row(D_MODEL), _const_spec(g1.shape), _const_spec(wm.shape), _const_spec(wlr.shape),
                  _const_spec(w2p.shape), _const_spec(gb.shape), _const_spec(qg.shape),
                  _const_spec(kg.shape), _const_spec(avg.shape)],
        out_specs=out_specs,
        out_shape=out_shape,
        compiler_params=pltpu.CompilerParams(dimension_semantics=("arbitrary",),
                                             vmem_limit_bytes=VMEM_LIMIT_BYTES),
        name="in_projection",
    )(x2, g1, wm, wlr, w2p, gb, qg, kg, avg)


def _sb_kernel(q_ref, k_ref, v_ref, mt_ref, o_ref, acc_ref):
    i = pl.program_id(2)
    tq, tk = SB_TQ, SB_TK
    q = q_ref[0]
    lane = lax.broadcasted_iota(jnp.int32, q.shape, 1)
    zero = jnp.zeros_like(q)
    q_heads = (jnp.where(lane < SB_HEAD_DIM, q, zero), jnp.where(lane >= SB_HEAD_DIM, q, zero))
    mt = mt_ref[...]
    key_idx = lax.broadcasted_iota(jnp.int32, (tk, tq), 0)
    qry_idx = lax.broadcasted_iota(jnp.int32, (tk, tq), 1)
    causal = key_idx < qry_idx

    acc_ref[...] = jnp.zeros_like(acc_ref)

    def block(j, carry, masked):
        start = pl.multiple_of(j * tk, tk)
        kb = k_ref[0, pl.ds(start, tk), :]
        vb = v_ref[0, pl.ds(start, tk), :]
        new_carry = []
        for h in range(2):
            zt = lax.dot_general(kb, q_heads[h], (((1,), (1,)), ((), ())),
                                 preferred_element_type=F32)
            lg = jnp.minimum(-zt, 0.0) - jnp.log(1.0 + jnp.exp(-jnp.abs(zt)))
            if masked:
                lg = jnp.where(causal, lg, 0.0)
            lgb = lg.astype(BF16)
            cs = jnp.dot(mt, lgb, preferred_element_type=F32)
            after = cs + carry[h]
            w = jnp.exp(zt + lg + after)
            if masked:
                w = jnp.where(causal, w, 0.0)
            acc_ref[h] += lax.dot_general(vb, w.astype(BF16), (((0,), (0,)), ((), ())),
                                          preferred_element_type=F32)
            new_carry.append(carry[h] + cs[0:1, :] + lgb[0:1, :].astype(F32))
        return tuple(new_carry)

    zero_row = jnp.zeros((1, tq), F32)
    carry = block(i, (zero_row, zero_row), True)

    def body(t, carry):
        return block(i - 1 - t, carry, False)

    lax.fori_loop(0, i, body, carry)

    row = lax.broadcasted_iota(jnp.int32, (LANES, tq), 0)
    ot = jnp.where(row < SB_HEAD_DIM, acc_ref[0], acc_ref[1])
    o_ref[0] = ot.T.astype(BF16)


def _sb_attention(q, k, v, mt):
    b, s, _ = q.shape
    tq = SB_TQ
    qspec = pl.BlockSpec((1, tq, LANES), lambda bi, hp, i: (bi, i, hp))
    kvspec = pl.BlockSpec((1, s, LANES), lambda bi, hp, i: (bi, 0, hp))
    return pl.pallas_call(
        _sb_kernel,
        grid=(b, SB_WIDTH // LANES, s // tq),
        in_specs=[qspec, kvspec, kvspec, _const_spec(mt.shape)],
        out_specs=qspec,
        out_shape=jax.ShapeDtypeStruct((b, s, SB_WIDTH), BF16),
        scratch_shapes=[pltpu.VMEM((2, LANES, tq), F32)],
        compiler_params=pltpu.CompilerParams(
            dimension_semantics=("arbitrary", "arbitrary", "arbitrary"),
            vmem_limit_bytes=VMEM_LIMIT_BYTES),
        name="stick_breaking_attention",
    )(q, k, v, mt)


def _gla_level_matrix():
    c = GLA_CHUNK
    tril = np.tril(np.ones((c, c), np.float32))
    mats = [tril]
    rows = np.arange(c)
    for n in GLA_LEVELS:
        base = (rows // n) * n
        second = (rows // n) % 2 == 1
        idx = np.where(second, base - 1, base + n - 1)
        mats.append(tril[idx])
    return np.concatenate(mats, axis=0)


def _gla_kernel(q_ref, k_ref, v_ref, la_ref, r_ref, cm_ref, ng_ref, o_ref, state_ref):
    c = GLA_CHUNK

    @pl.when(pl.program_id(2) == 0)
    def _():
        state_ref[...] = jnp.zeros_like(state_ref)

    cm = cm_ref[...]
    row128 = lax.broadcasted_iota(jnp.int32, (c, GLA_DK), 0)
    ri = lax.broadcasted_iota(jnp.int32, (c, c), 0)
    ci = lax.broadcasted_iota(jnp.int32, (c, c), 1)
    nt = (((1,), (1,)), ((), ()))

    for ch in range(GLA_STEP // c):
        sl = slice(ch * c, (ch + 1) * c)
        g = la_ref[0, sl, :]
        ghi = g.astype(BF16)
        glo = (g - ghi.astype(F32)).astype(BF16)
        st = (jnp.dot(cm, ghi, preferred_element_type=F32)
              + jnp.dot(cm, glo, preferred_element_type=F32))
        cum = st[0:c]
        last = cum[c - 1:c, :]
        qb = q_ref[0, sl, :]
        kb = k_ref[0, sl, :]
        vb = v_ref[0, sl, :]
        qf = qb.astype(F32)
        kf = kb.astype(F32)
        state = state_ref[...]

        o = jnp.dot((qf * jnp.exp(cum)).astype(BF16), state.astype(BF16), preferred_element_type=F32)

        sc = jnp.where(ri == ci, lax.dot_general(qb, kb, nt, preferred_element_type=F32), 0.0)
        for li, n in enumerate(GLA_LEVELS):
            sh = n.bit_length() - 1
            d = cum - st[(li + 1) * c:(li + 2) * c]
            second = ((row128 >> sh) & 1) == 1
            ex = jnp.exp(jnp.where(second, d, -d))
            ql = jnp.where(second, qf * ex, 0.0).astype(BF16)
            kl = jnp.where(second, 0.0, kf * ex).astype(BF16)
            s_l = lax.dot_general(ql, kl, nt, preferred_element_type=F32)
            sc = sc + jnp.where((ri >> (sh + 1)) == (ci >> (sh + 1)), s_l, 0.0)
        o = o + jnp.dot(sc.astype(BF16), vb, preferred_element_type=F32)

        kd = (kf * jnp.exp(last - cum)).astype(BF16)
        decay = jnp.broadcast_to(jnp.exp(last), (GLA_DK, GLA_DK)).T
        upd = lax.dot_general(kd, vb, (((0,), (0,)), ((), ())), preferred_element_type=F32)
        state_ref[...] = state * jnp.concatenate([decay, decay], axis=1) + upd

        ms = jnp.mean(o * o, axis=-1, keepdims=True)
        y = o * lax.rsqrt(ms + RMS_EPS) * ng_ref[...]
        o_ref[0, sl, :] = (y * r_ref[0, sl, :].astype(F32)).astype(BF16)


def _gla(gq, gk, gv, la, sr, cm, ng):
    b, s, _ = gq.shape
    ts = GLA_STEP
    kspec = pl.BlockSpec((1, ts, GLA_DK), lambda bi, h, t: (bi, t, h))
    vspec = pl.BlockSpec((1, ts, GLA_DV), lambda bi, h, t: (bi, t, h))
    return pl.pallas_call(
        _gla_kernel,
        grid=(b, GLA_HEADS, s // ts),
        in_specs=[kspec, kspec, vspec, kspec, vspec, _const_spec(cm.shape), _const_spec(ng.shape)],
        out_specs=vspec,
        out_shape=jax.ShapeDtypeStruct((b, s, GLA_VALUE_WIDTH), BF16),
        scratch_shapes=[pltpu.VMEM((GLA_DK, GLA_DV), F32)],
        compiler_params=pltpu.CompilerParams(
            dimension_semantics=("arbitrary", "arbitrary", "arbitrary"),
            vmem_limit_bytes=VMEM_LIMIT_BYTES),
        name="gated_linear_attention",
    )(gq, gk, gv, la, sr, cm, ng)


def _tail_kernel(x_ref, osb_ref, ogla_ref, gs_ref, gg_ref, wsb_ref, wgla_ref, wout_ref,
                 g2_ref, wg_ref, wu_ref, wd_ref, o_ref):
    a = jnp.dot(osb_ref[...], wsb_ref[...], preferred_element_type=F32)
    b = jnp.dot(ogla_ref[...], wgla_ref[...], preferred_element_type=F32)
    merged = gs_ref[...].astype(F32) * a + gg_ref[...].astype(F32) * b
    x1 = x_ref[...] + jnp.dot(merged.astype(BF16), wout_ref[...], preferred_element_type=F32)
    ms = jnp.mean(x1 * x1, axis=-1, keepdims=True)
    h2 = (x1 * lax.rsqrt(ms + RMS_EPS) * g2_ref[...]).astype(BF16)
    gate = jnp.dot(h2, wg_ref[...], preferred_element_type=F32)
    up = jnp.dot(h2, wu_ref[...], preferred_element_type=F32)
    hid = (gate * jax.nn.sigmoid(gate) * up).astype(BF16)
    o_ref[...] = x1 + jnp.dot(hid, wd_ref[...], preferred_element_type=F32)


def _tail(x2, osb, ogla, gs, gg, wsb, wgla, wout, g2, wg, wu, wd):
    t = x2.shape[0]
    tm = ROW_TILE
    row = pl.BlockSpec((tm, D_MODEL), lambda i: (i, 0))
    consts = (wsb, wgla, wout, g2, wg, wu, wd)
    return pl.pallas_call(
        _tail_kernel,
        grid=(t // tm,),
        in_specs=[row] * 5 + [_const_spec(c.shape) for c in consts],
        out_specs=row,
        out_shape=jax.ShapeDtypeStruct((t, D_MODEL), F32),
        compiler_params=pltpu.CompilerParams(dimension_semantics=("arbitrary",),
                                             vmem_limit_bytes=VMEM_LIMIT_BYTES),
        name="merge_out_ffn",
    )(x2, osb, ogla, gs, gg, *consts)


def _layer(x, norm1_g, w_in, sb_q_norm_g, sb_k_norm_g, gla_gate_w2, gla_gate_b, gla_out_norm_g,
           w_branch_sb, w_branch_gla, w_out, norm2_g, w_ffn_gate, w_ffn_up, w_ffn_down):
    b, s, d = x.shape
    t = b * s
    x2 = x.reshape(t, d)

    lr0 = 3 * SB_WIDTH + 2 * GLA_KEY_WIDTH + 2 * GLA_VALUE_WIDTH
    lr1 = lr0 + GLA_GATE_RANK
    wm = jnp.concatenate([w_in[:, :lr0], w_in[:, lr1:]], axis=1).astype(BF16)
    wlr = jnp.pad(w_in[:, lr0:lr1], ((0, 0), (0, LANES - GLA_GATE_RANK))).astype(BF16)
    w2p = jnp.pad(gla_gate_w2, ((0, LANES - GLA_GATE_RANK), (0, 0))).astype(BF16)

    heads_per_tile = MXU_DIM // SB_HEAD_DIM
    qg = jnp.tile(sb_q_norm_g * (SB_HEAD_DIM ** -0.5), heads_per_tile).reshape(1, MXU_DIM)
    kg = jnp.tile(sb_k_norm_g, heads_per_tile).reshape(1, MXU_DIM)
    hid = np.arange(MXU_DIM) // SB_HEAD_DIM
    avg = jnp.asarray((hid[:, None] == hid[None, :]).astype(np.float32) / SB_HEAD_DIM, BF16)

    q, k, v, gq, gk, gv, sr, la, gs, gg = _in_projection(
        x2, norm1_g.reshape(1, d), wm, wlr, w2p, gla_gate_b.reshape(1, -1), qg, kg, avg)

    idx = np.arange(SB_TK)
    mt = jnp.asarray((idx[None, :] > idx[:, None]).astype(np.float32), BF16)
    o_sb = _sb_attention(q.reshape(b, s, -1), k.reshape(b, s, -1), v.reshape(b, s, -1), mt)

    cm = jnp.asarray(_gla_level_matrix(), BF16)
    o_gla = _gla(gq.reshape(b, s, -1), gk.reshape(b, s, -1), gv.reshape(b, s, -1),
                 la.reshape(b, s, -1), sr.reshape(b, s, -1), cm, gla_out_norm_g.reshape(1, -1))

    out = _tail(x2, o_sb.reshape(t, -1), o_gla.reshape(t, -1), gs, gg,
                w_branch_sb.astype(BF16), w_branch_gla.astype(BF16), w_out.astype(BF16),
                norm2_g.reshape(1, d), w_ffn_gate.astype(BF16), w_ffn_up.astype(BF16),
                w_ffn_down.astype(BF16))
    return out.reshape(b, s, d)


def kernel(x, norm1_g, w_in, sb_q_norm_g, sb_k_norm_g, gla_gate_w2, gla_gate_b, gla_out_norm_g,
           w_branch_sb, w_branch_gla, w_out, norm2_g, w_ffn_gate, w_ffn_up, w_ffn_down):
    for l in range(norm1_g.shape[0]):
        x = _layer(x, norm1_g[l], w_in[l], sb_q_norm_g[l], sb_k_norm_g[l], gla_gate_w2[l],
                   gla_gate_b[l], gla_out_norm_g[l], w_branch_sb[l], w_branch_gla[l], w_out[l],
                   norm2_g[l], w_ffn_gate[l], w_ffn_up[l], w_ffn_down[l])
    return x
```

```python
import functools

import jax
import jax.numpy as jnp
import numpy as np
from jax import lax
from jax.experimental import pallas as pl
from jax.experimental.pallas import tpu as pltpu

F32 = jnp.float32
BF16 = jnp.bfloat16

D_MODEL = 1024
SB_HEADS = 16
SB_HEAD_DIM = 64
SB_WIDTH = SB_HEADS * SB_HEAD_DIM
GLA_HEADS = 4
GLA_KEY_WIDTH = D_MODEL // 2
GLA_VALUE_WIDTH = D_MODEL
GLA_DK = GLA_KEY_WIDTH // GLA_HEADS
GLA_DV = GLA_VALUE_WIDTH // GLA_HEADS
GLA_GATE_RANK = 16
GLA_GATE_TAU = 16.0
D_FF = -(-8 * D_MODEL // (3 * 256)) * 256
RMS_EPS = 1e-6
LOG2E = 1.4426950408889634

LANES = 128
MXU_DIM = 256
VMEM_LIMIT_BYTES = 56 * 1024 * 1024

ROW_TILE = 256
SB_TQ = 1024
SB_TK = 256
GLA_CHUNK = 64
GLA_STEP = 512
GLA_LEVELS = (32, 16, 8, 4, 2, 1)


def _const_spec(shape):
    nd = len(shape)
    return pl.BlockSpec(shape, lambda *_: (0,) * nd, pipeline_mode=pl.Buffered(1))


def _log_sigmoid(x):
    return jnp.minimum(x, 0.0) - jnp.log(1.0 + jnp.exp(-jnp.abs(x)))


def _inproj_kernel(x_ref, g1_ref, wm_ref, wlr_ref, w2_ref, gb_ref, qg_ref, kg_ref, avg_ref,
                   q_ref, k_ref, v_ref, gq_ref, gk_ref, gv_ref, r_ref, la_ref, gs_ref, gg_ref):
    x = x_ref[...]
    ms = jnp.mean(x * x, axis=-1, keepdims=True)
    h = (x * lax.rsqrt(ms + RMS_EPS) * g1_ref[...]).astype(BF16)

    def proj(c0, cw):
        return jnp.dot(h, wm_ref[:, c0:c0 + cw], preferred_element_type=F32)

    def head_norm(y, gain):
        m = jnp.dot((y * y).astype(BF16), avg_ref[...], preferred_element_type=F32)
        return y * lax.rsqrt(m + RMS_EPS) * gain

    w = MXU_DIM
    for b in range(SB_WIDTH // w):
        q_ref[:, b * w:(b + 1) * w] = head_norm(proj(b * w, w), qg_ref[...]).astype(BF16)
    off = SB_WIDTH
    for b in range(SB_WIDTH // w):
        k_ref[:, b * w:(b + 1) * w] = head_norm(proj(off + b * w, w), kg_ref[...]).astype(BF16)
    off += SB_WIDTH
    cw = 512
    for b in range(SB_WIDTH // cw):
        v_ref[:, b * cw:(b + 1) * cw] = proj(off + b * cw, cw).astype(BF16)
    off += SB_WIDTH
    gq_ref[...] = (proj(off, GLA_KEY_WIDTH) * (GLA_DK ** -0.5)).astype(BF16)
    off += GLA_KEY_WIDTH
    gk_ref[...] = proj(off, GLA_KEY_WIDTH).astype(BF16)
    off += GLA_KEY_WIDTH
    for b in range(GLA_VALUE_WIDTH // cw):
        gv_ref[:, b * cw:(b + 1) * cw] = proj(off + b * cw, cw).astype(BF16)
    off += GLA_VALUE_WIDTH
    for b in range(GLA_VALUE_WIDTH // cw):
        r = proj(off + b * cw, cw)
        r_ref[:, b * cw:(b + 1) * cw] = (r * jax.nn.sigmoid(r)).astype(BF16)
    off += GLA_VALUE_WIDTH
    for b in range(D_MODEL // cw):
        gs_ref[:, b * cw:(b + 1) * cw] = jax.nn.sigmoid(proj(off + b * cw, cw)).astype(BF16)
    off += D_MODEL
    for b in range(D_MODEL // cw):
        gg_ref[:, b * cw:(b + 1) * cw] = jax.nn.sigmoid(proj(off + b * cw, cw)).astype(BF16)

    lr = jnp.dot(h, wlr_ref[...], preferred_element_type=F32)
    logit = jnp.dot(lr.astype(BF16), w2_ref[...], preferred_element_type=F32) + gb_ref[...]
    la_ref[...] = _log_sigmoid(logit) * (1.0 / GLA_GATE_TAU)


def _in_projection(x2, g1, wm, wlr, w2p, gb, qg, kg, avg):
    t = x2.shape[0]
    tm = ROW_TILE
    row = lambda width: pl.BlockSpec((tm, width), lambda i: (i, 0))
    bf = lambda width: jax.ShapeDtypeStruct((t, width), BF16)
    out_shape = (bf(SB_WIDTH), bf(SB_WIDTH), bf(SB_WIDTH),
                 bf(GLA_KEY_WIDTH), bf(GLA_KEY_WIDTH), bf(GLA_VALUE_WIDTH), bf(GLA_VALUE_WIDTH),
                 jax.ShapeDtypeStruct((t, GLA_KEY_WIDTH), F32), bf(D_MODEL), bf(D_MODEL))
    out_specs = (row(SB_WIDTH), row(SB_WIDTH), row(SB_WIDTH),
                 row(GLA_KEY_WIDTH), row(GLA_KEY_WIDTH), row(GLA_VALUE_WIDTH), row(GLA_VALUE_WIDTH),
                 row(GLA_KEY_WIDTH), row(D_MODEL), row(D_MODEL))
    return pl.pallas_call(
        _inproj_kernel,
        grid=(t // tm,),
        in_specs=[row(D_MODEL), _const_spec(g1.shape), _const_spec(wm.shape), _const_spec(wlr.shape),
                  _const_spec(w2p.shape), _const_spec(gb.shape), _const_spec(qg.shape),
                  _const_spec(kg.shape), _const_spec(avg.shape)],
        out_specs=out_specs,
        out_shape=out_shape,
        compiler_params=pltpu.CompilerParams(dimension_semantics=("arbitrary",),
                                             vmem_limit_bytes=VMEM_LIMIT_BYTES),
        name="in_projection",
    )(x2, g1, wm, wlr, w2p, gb, qg, kg, avg)


def _neg_abs(x):
    sign = jnp.uint16(0x8000)
    return lax.bitcast_convert_type(lax.bitcast_convert_type(x, jnp.uint16) | sign, BF16)


def _sb_kernel(q_ref, k_ref, v_ref, mt_ref, o_ref, acc_ref, carry_ref):
    qi = pl.program_id(2)
    tq, tk = SB_TQ, SB_TK
    sub = tq // tk
    q = q_ref[0]
    lane = lax.broadcasted_iota(jnp.int32, q.shape, 1)
    zero = jnp.zeros_like(q)
    q_heads = (jnp.where(lane < SB_HEAD_DIM, q, zero), jnp.where(lane >= SB_HEAD_DIM, q, zero))
    mt = mt_ref[...]

    acc_ref[...] = jnp.zeros_like(acc_ref)
    carry_ref[...] = jnp.zeros_like(carry_ref)

    def block(j, c0, masked):
        wd = tq - c0
        start = pl.multiple_of(j * tk, tk)
        kb = k_ref[0, pl.ds(start, tk), :]
        vb = v_ref[0, pl.ds(start, tk), :]
        if masked:
            causal = (lax.broadcasted_iota(jnp.int32, (tk, wd), 0)
                      < lax.broadcasted_iota(jnp.int32, (tk, wd), 1))
        one = jnp.ones((), BF16)
        zero16 = jnp.zeros((), BF16)
        zts = [lax.dot_general(kb, q_heads[h][c0:, :], (((1,), (1,)), ((), ())),
                               preferred_element_type=F32).astype(BF16)
               for h in range(2)]
        ps, css = [], []
        for h in range(2):
            zt = zts[h]
            l2 = jnp.log2((one + jnp.exp2(_neg_abs(zt))).astype(F32)).astype(BF16)
            p = jnp.maximum(zt, zero16) + l2
            if masked:
                p = jnp.where(causal, p, zero16)
            ps.append(p)
            css.append(jnp.dot(mt, p, preferred_element_type=F32))
        for h in range(2):
            zt, p, cs = zts[h], ps[h], css[h]
            carry = carry_ref[h, :, c0:]
            w = jnp.exp2(((zt - p) + cs.astype(BF16)) + carry.astype(BF16))
            if masked:
                w = jnp.where(causal, w, zero16)
            acc_ref[h, :, c0:] += lax.dot_general(vb, w, (((0,), (0,)), ((), ())),
                                                  preferred_element_type=F32)
            carry_ref[h, :, c0:] = carry + cs[0:1, :] - p[0:1, :].astype(F32)

    for jj in reversed(range(sub)):
        block(qi * sub + jj, jj * tk, True)

    def body(t, _):
        block(qi * sub - 1 - t, 0, False)
        return 0

    lax.fori_loop(0, qi * sub, body, 0)

    row = lax.broadcasted_iota(jnp.int32, (LANES, tq), 0)
    ot = jnp.where(row < SB_HEAD_DIM, acc_ref[0], acc_ref[1])
    o_ref[0] = ot.T.astype(BF16)


def _sb_attention(q, k, v, mt):
    b, s, _ = q.shape
    tq = SB_TQ
    qspec = pl.BlockSpec((1, tq, LANES), lambda bi, hp, i: (bi, i, hp))
    kvspec = pl.BlockSpec((1, s, LANES), lambda bi, hp, i: (bi, 0, hp))
    return pl.pallas_call(
        _sb_kernel,
        grid=(b, SB_WIDTH // LANES, s // tq),
        in_specs=[qspec, kvspec, kvspec, _const_spec(mt.shape)],
        out_specs=qspec,
        out_shape=jax.ShapeDtypeStruct((b, s, SB_WIDTH), BF16),
        scratch_shapes=[pltpu.VMEM((2, LANES, tq), F32), pltpu.VMEM((2, 1, tq), F32)],
        compiler_params=pltpu.CompilerParams(
            dimension_semantics=("arbitrary", "arbitrary", "arbitrary"),
            vmem_limit_bytes=VMEM_LIMIT_BYTES),
        name="stick_breaking_attention",
    )(q, k, v, mt)


def _gla_level_matrix():
    c = GLA_CHUNK
    tril = np.tril(np.ones((c, c), np.float32))
    mats = [tril]
    rows = np.arange(c)
    for n in GLA_LEVELS:
        base = (rows // n) * n
        second = (rows // n) % 2 == 1
        idx = np.where(second, base - 1, base + n - 1)
        mats.append(tril[idx])
    return np.concatenate(mats, axis=0)


def _gla_kernel(q_ref, k_ref, v_ref, la_ref, r_ref, cm_ref, ng_ref, o_ref, state_ref):
    c = GLA_CHUNK

    @pl.when(pl.program_id(2) == 0)
    def _():
        state_ref[...] = jnp.zeros_like(state_ref)

    cm = cm_ref[...]
    row128 = lax.broadcasted_iota(jnp.int32, (c, GLA_DK), 0)
    ri = lax.broadcasted_iota(jnp.int32, (c, c), 0)
    ci = lax.broadcasted_iota(jnp.int32, (c, c), 1)
    nt = (((1,), (1,)), ((), ()))

    for ch in range(GLA_STEP // c):
        sl = slice(ch * c, (ch + 1) * c)
        g = la_ref[0, sl, :]
        ghi = g.astype(BF16)
        glo = (g - ghi.astype(F32)).astype(BF16)
        st = (jnp.dot(cm, ghi, preferred_element_type=F32)
              + jnp.dot(cm, glo, preferred_element_type=F32))
        cum = st[0:c]
        last = cum[c - 1:c, :]
        qb = q_ref[0, sl, :]
        kb = k_ref[0, sl, :]
        vb = v_ref[0, sl, :]
        qf = qb.astype(F32)
        kf = kb.astype(F32)
        state = state_ref[...]

        o = jnp.dot((qf * jnp.exp(cum)).astype(BF16), state.astype(BF16), preferred_element_type=F32)

        sc = jnp.where(ri == ci, lax.dot_general(qb, kb, nt, preferred_element_type=F32), 0.0)
        for li, n in enumerate(GLA_LEVELS):
            sh = n.bit_length() - 1
            d = cum - st[(li + 1) * c:(li + 2) * c]
            second = ((row128 >> sh) & 1) == 1
            ex = jnp.exp(jnp.where(second, d, -d))
            ql = jnp.where(second, qf * ex, 0.0).astype(BF16)
            kl = jnp.where(second, 0.0, kf * ex).astype(BF16)
            s_l = lax.dot_general(ql, kl, nt, preferred_element_type=F32)
            sc = sc + jnp.where((ri >> (sh + 1)) == (ci >> (sh + 1)), s_l, 0.0)
        o = o + jnp.dot(sc.astype(BF16), vb, preferred_element_type=F32)

        kd = (kf * jnp.exp(last - cum)).astype(BF16)
        decay = jnp.broadcast_to(jnp.exp(last), (GLA_DK, GLA_DK)).T
        upd = lax.dot_general(kd, vb, (((0,), (0,)), ((), ())), preferred_element_type=F32)
        state_ref[...] = state * jnp.concatenate([decay, decay], axis=1) + upd

        ms = jnp.mean(o * o, axis=-1, keepdims=True)
        y = o * lax.rsqrt(ms + RMS_EPS) * ng_ref[...]
        o_ref[0, sl, :] = (y * r_ref[0, sl, :].astype(F32)).astype(BF16)


def _gla(gq, gk, gv, la, sr, cm, ng):
    b, s, _ = gq.shape
    ts = GLA_STEP
    kspec = pl.BlockSpec((1, ts, GLA_DK), lambda bi, h, t: (bi, t, h))
    vspec = pl.BlockSpec((1, ts, GLA_DV), lambda bi, h, t: (bi, t, h))
    return pl.pallas_call(
        _gla_kernel,
        grid=(b, GLA_HEADS, s // ts),
        in_specs=[kspec, kspec, vspec, kspec, vspec, _const_spec(cm.shape), _const_spec(ng.shape)],
        out_specs=vspec,
        out_shape=jax.ShapeDtypeStruct((b, s, GLA_VALUE_WIDTH), BF16),
        scratch_shapes=[pltpu.VMEM((GLA_DK, GLA_DV), F32)],
        compiler_params=pltpu.CompilerParams(
            dimension_semantics=("arbitrary", "arbitrary", "arbitrary"),
            vmem_limit_bytes=VMEM_LIMIT_BYTES),
        name="gated_linear_attention",
    )(gq, gk, gv, la, sr, cm, ng)


def _tail_kernel(x_ref, osb_ref, ogla_ref, gs_ref, gg_ref, wsb_ref, wgla_ref, wout_ref,
                 g2_ref, wg_ref, wu_ref, wd_ref, o_ref):
    a = jnp.dot(osb_ref[...], wsb_ref[...], preferred_element_type=F32)
    b = jnp.dot(ogla_ref[...], wgla_ref[...], preferred_element_type=F32)
    merged = gs_ref[...].astype(F32) * a + gg_ref[...].astype(F32) * b
    x1 = x_ref[...] + jnp.dot(merged.astype(BF16), wout_ref[...], preferred_element_type=F32)
    ms = jnp.mean(x1 * x1, axis=-1, keepdims=True)
    h2 = (x1 * lax.rsqrt(ms + RMS_EPS) * g2_ref[...]).astype(BF16)
    gate = jnp.dot(h2, wg_ref[...], preferred_element_type=F32)
    up = jnp.dot(h2, wu_ref[...], preferred_element_type=F32)
    hid = (gate * jax.nn.sigmoid(gate) * up).astype(BF16)
    o_ref[...] = x1 + jnp.dot(hid, wd_ref[...], preferred_element_type=F32)


def _tail(x2, osb, ogla, gs, gg, wsb, wgla, wout, g2, wg, wu, wd):
    t = x2.shape[0]
    tm = ROW_TILE
    row = pl.BlockSpec((tm, D_MODEL), lambda i: (i, 0))
    consts = (wsb, wgla, wout, g2, wg, wu, wd)
    return pl.pallas_call(
        _tail_kernel,
        grid=(t // tm,),
        in_specs=[row] * 5 + [_const_spec(c.shape) for c in consts],
        out_specs=row,
        out_shape=jax.ShapeDtypeStruct((t, D_MODEL), F32),
        compiler_params=pltpu.CompilerParams(dimension_semantics=("arbitrary",),
                                             vmem_limit_bytes=VMEM_LIMIT_BYTES),
        name="merge_out_ffn",
    )(x2, osb, ogla, gs, gg, *consts)


def _layer(x, norm1_g, w_in, sb_q_norm_g, sb_k_norm_g, gla_gate_w2, gla_gate_b, gla_out_norm_g,
           w_branch_sb, w_branch_gla, w_out, norm2_g, w_ffn_gate, w_ffn_up, w_ffn_down):
    b, s, d = x.shape
    t = b * s
    x2 = x.reshape(t, d)

    lr0 = 3 * SB_WIDTH + 2 * GLA_KEY_WIDTH + 2 * GLA_VALUE_WIDTH
    lr1 = lr0 + GLA_GATE_RANK
    wm = jnp.concatenate([w_in[:, :lr0], w_in[:, lr1:]], axis=1).astype(BF16)
    wlr = jnp.pad(w_in[:, lr0:lr1], ((0, 0), (0, LANES - GLA_GATE_RANK))).astype(BF16)
    w2p = jnp.pad(gla_gate_w2, ((0, LANES - GLA_GATE_RANK), (0, 0))).astype(BF16)

    heads_per_tile = MXU_DIM // SB_HEAD_DIM
    qg = jnp.tile(sb_q_norm_g * (SB_HEAD_DIM ** -0.5 * LOG2E), heads_per_tile).reshape(1, MXU_DIM)
    kg = jnp.tile(sb_k_norm_g, heads_per_tile).reshape(1, MXU_DIM)
    hid = np.arange(MXU_DIM) // SB_HEAD_DIM
    avg = jnp.asarray((hid[:, None] == hid[None, :]).astype(np.float32) / SB_HEAD_DIM, BF16)

    q, k, v, gq, gk, gv, sr, la, gs, gg = _in_projection(
        x2, norm1_g.reshape(1, d), wm, wlr, w2p, gla_gate_b.reshape(1, -1), qg, kg, avg)

    idx = np.arange(SB_TK)
    mt = jnp.asarray(-(idx[None, :] > idx[:, None]).astype(np.float32), BF16)
    o_sb = _sb_attention(q.reshape(b, s, -1), k.reshape(b, s, -1), v.reshape(b, s, -1), mt)

    cm = jnp.asarray(_gla_level_matrix(), BF16)
    o_gla = _gla(gq.reshape(b, s, -1), gk.reshape(b, s, -1), gv.reshape(b, s, -1),
                 la.reshape(b, s, -1), sr.reshape(b, s, -1), cm, gla_out_norm_g.reshape(1, -1))

    out = _tail(x2, o_sb.reshape(t, -1), o_gla.reshape(t, -1), gs, gg,
                w_branch_sb.astype(BF16), w_branch_gla.astype(BF16), w_out.astype(BF16),
                norm2_g.reshape(1, d), w_ffn_gate.astype(BF16), w_ffn_up.astype(BF16),
                w_ffn_down.astype(BF16))
    return out.reshape(b, s, d)


def kernel(x, norm1_g, w_in, sb_q_norm_g, sb_k_norm_g, gla_gate_w2, gla_gate_b, gla_out_norm_g,
           w_branch_sb, w_branch_gla, w_out, norm2_g, w_ffn_gate, w_ffn_up, w_ffn_down):
    for l in range(norm1_g.shape[0]):
        x = _layer(x, norm1_g[l], w_in[l], sb_q_norm_g[l], sb_k_norm_g[l], gla_gate_w2[l],
                   gla_gate_b[l], gla_out_norm_g[l], w_branch_sb[l], w_branch_gla[l], w_out[l],
                   norm2_g[l], w_ffn_gate[l], w_ffn_up[l], w_ffn_down[l])
    return x
```

```python
import functools

import jax
import jax.numpy as jnp
import numpy as np
from jax import lax
from jax.experimental import pallas as pl
from jax.experimental.pallas import tpu as pltpu

F32 = jnp.float32
BF16 = jnp.bfloat16

D_MODEL = 1024
SB_HEADS = 16
SB_HEAD_DIM = 64
SB_WIDTH = SB_HEADS * SB_HEAD_DIM
GLA_HEADS = 4
GLA_KEY_WIDTH = D_MODEL // 2
GLA_VALUE_WIDTH = D_MODEL
GLA_DK = GLA_KEY_WIDTH // GLA_HEADS
GLA_DV = GLA_VALUE_WIDTH // GLA_HEADS
GLA_GATE_RANK = 16
GLA_GATE_TAU = 16.0
D_FF = -(-8 * D_MODEL // (3 * 256)) * 256
RMS_EPS = 1e-6
LOG2E = 1.4426950408889634

LANES = 128
MXU_DIM = 256
VMEM_LIMIT_BYTES = 56 * 1024 * 1024

ROW_TILE = 256
INPROJ_ROW_TILE = 512
SB_TQ = 1024
SB_TK = 256
SB_UNDERFLOW_LOG2 = -150.0
SB_NO_BLOCK_BIAS = -1e30
GLA_CHUNK = 64
GLA_STEP = 512
GLA_LEVELS = (32, 16, 8, 4, 2, 1)


def _const_spec(shape):
    nd = len(shape)
    return pl.BlockSpec(shape, lambda *_: (0,) * nd, pipeline_mode=pl.Buffered(1))


def _log_sigmoid(x):
    return jnp.minimum(x, 0.0) - jnp.log(1.0 + jnp.exp(-jnp.abs(x)))


def _inproj_kernel(x_ref, g1_ref, wm_ref, wlr_ref, w2_ref, gb_ref, qg_ref, kg_ref, avg_ref,
                   q_ref, k_ref, v_ref, gq_ref, gk_ref, gv_ref, r_ref, la_ref, gs_ref, gg_ref):
    x = x_ref[...]
    ms = jnp.mean(x * x, axis=-1, keepdims=True)
    h = (x * lax.rsqrt(ms + RMS_EPS) * g1_ref[...]).astype(BF16)

    def proj(c0, cw):
        return jnp.dot(h, wm_ref[:, c0:c0 + cw], preferred_element_type=F32)

    def head_norm(y, gain):
        m = jnp.dot((y * y).astype(BF16), avg_ref[...], preferred_element_type=F32)
        return y * lax.rsqrt(m + RMS_EPS) * gain

    segments = (
        (q_ref, SB_WIDTH, MXU_DIM, lambda y: head_norm(y, qg_ref[...])),
        (k_ref, SB_WIDTH, MXU_DIM, lambda y: head_norm(y, kg_ref[...])),
        (v_ref, SB_WIDTH, 512, lambda y: y),
        (gq_ref, GLA_KEY_WIDTH, 512, lambda y: y * (GLA_DK ** -0.5)),
        (gk_ref, GLA_KEY_WIDTH, 512, lambda y: y),
        (gv_ref, GLA_VALUE_WIDTH, 512, lambda y: y),
        (r_ref, GLA_VALUE_WIDTH, 512, lambda y: y * jax.nn.sigmoid(y)),
        (gs_ref, D_MODEL, 512, jax.nn.sigmoid),
        (gg_ref, D_MODEL, 512, jax.nn.sigmoid),
    )
    jobs, off = [], 0
    for ref, width, cw, post in segments:
        jobs += [(ref, lo, off + lo, cw, post) for lo in range(0, width, cw)]
        off += width

    lr = jnp.dot(h, wlr_ref[...], preferred_element_type=F32)
    y_next = proj(jobs[0][2], jobs[0][3])
    for n, (ref, lo, _, cw, post) in enumerate(jobs):
        y = y_next
        if n + 1 < len(jobs):
            y_next = proj(jobs[n + 1][2], jobs[n + 1][3])
        if n == 0:
            logit = jnp.dot(lr.astype(BF16), w2_ref[...], preferred_element_type=F32) + gb_ref[...]
        ref[:, lo:lo + cw] = post(y).astype(BF16)
    la_ref[...] = _log_sigmoid(logit) * (1.0 / GLA_GATE_TAU)


def _in_projection(x2, g1, wm, wlr, w2p, gb, qg, kg, avg):
    t = x2.shape[0]
    tm = INPROJ_ROW_TILE
    row = lambda width: pl.BlockSpec((tm, width), lambda i: (i, 0))
    bf = lambda width: jax.ShapeDtypeStruct((t, width), BF16)
    out_shape = (bf(SB_WIDTH), bf(SB_WIDTH), bf(SB_WIDTH),
                 bf(GLA_KEY_WIDTH), bf(GLA_KEY_WIDTH), bf(GLA_VALUE_WIDTH), bf(GLA_VALUE_WIDTH),
                 jax.ShapeDtypeStruct((t, GLA_KEY_WIDTH), F32), bf(D_MODEL), bf(D_MODEL))
    out_specs = (row(SB_WIDTH), row(SB_WIDTH), row(SB_WIDTH),
                 row(GLA_KEY_WIDTH), row(GLA_KEY_WIDTH), row(GLA_VALUE_WIDTH), row(GLA_VALUE_WIDTH),
                 row(GLA_KEY_WIDTH), row(D_MODEL), row(D_MODEL))
    return pl.pallas_call(
        _inproj_kernel,
        grid=(t // tm,),
        in_specs=[row(D_MODEL), _const_spec(g1.shape), _const_spec(wm.shape), _const_spec(wlr.shape),
                  _const_spec(w2p.shape), _const_spec(gb.shape), _const_spec(qg.shape),
                  _const_spec(kg.shape), _const_spec(avg.shape)],
        out_specs=out_specs,
        out_shape=out_shape,
        compiler_params=pltpu.CompilerParams(dimension_semantics=("arbitrary",),
                                             vmem_limit_bytes=VMEM_LIMIT_BYTES),
        name="in_projection",
    )(x2, g1, wm, wlr, w2p, gb, qg, kg, avg)


def _sb_kernel(q_ref, k_ref, v_ref, mt_ref, o_ref, acc_ref, carry_ref):
    qi = pl.program_id(2)
    tq, tk = SB_TQ, SB_TK
    sub = tq // tk
    nt = (((1,), (1,)), ((), ()))
    tn = (((0,), (0,)), ((), ()))
    q = q_ref[0]
    lane = lax.broadcasted_iota(jnp.int32, q.shape, 1)
    zero = jnp.zeros_like(q)
    q_heads = (jnp.where(lane < SB_HEAD_DIM, q, zero), jnp.where(lane >= SB_HEAD_DIM, q, zero))
    mt = mt_ref[...]
    causal = (lax.broadcasted_iota(jnp.int32, (tk, tk), 0)
              < lax.broadcasted_iota(jnp.int32, (tk, tk), 1))
    one = jnp.ones((), BF16)
    zero16 = jnp.zeros((), BF16)

    acc_ref[...] = jnp.zeros_like(acc_ref)
    carry_ref[...] = jnp.zeros_like(carry_ref)

    def cols(g):
        return slice(g * tk, (g + 1) * tk)

    def logits(j, g):
        kb = k_ref[0, pl.ds(pl.multiple_of(j * tk, tk), tk), :]
        return [lax.dot_general(kb, q_heads[h][cols(g), :], nt,
                                preferred_element_type=F32).astype(BF16) for h in range(2)]

    def log_not(zts, masked):
        ps, css = [], []
        for zt in zts:
            l2 = jnp.log2((one + jnp.exp2(-jnp.abs(zt))).astype(F32)).astype(BF16)
            p = jnp.maximum(zt, zero16) + l2
            if masked:
                p = jnp.where(causal, p, zero16)
            ps.append(p)
            css.append(jnp.dot(mt, p, preferred_element_type=F32))
        return ps, css

    def weights(j, g, zts, ps, css, masked, bias=None):
        vb = v_ref[0, pl.ds(pl.multiple_of(j * tk, tk), tk), :]
        for h in range(2):
            carry = carry_ref[h, :, cols(g)]
            row = carry if bias is None else carry + bias
            w = jnp.exp2(((zts[h] - ps[h]) + css[h].astype(BF16)) + row.astype(BF16))
            if masked:
                w = jnp.where(causal, w, zero16)
            acc_ref[h, :, cols(g)] += lax.dot_general(vb, w, tn, preferred_element_type=F32)
            carry_ref[h, :, cols(g)] = carry + css[h][0:1, :] - ps[h][0:1, :].astype(F32)

    first = qi * sub
    units = [(first + g, g, True) for g in range(sub)] + [(first + g - 1, g, False) for g in range(sub)]
    no_prev_bias = jnp.where(first > 0, 0.0, SB_NO_BLOCK_BIAS).astype(F32)
    zs = {0: logits(jnp.maximum(units[0][0], 0), units[0][1])}
    pcs = {}
    for n, (j, g, masked) in enumerate(units):
        if n + 1 < len(units):
            zs[n + 1] = logits(jnp.maximum(units[n + 1][0], 0), units[n + 1][1])
        pcs[n] = log_not(zs[n], masked)
        if n >= 1:
            jp, gp, mp = units[n - 1]
            bias = no_prev_bias if (not mp and gp == 0) else None
            weights(jnp.maximum(jp, 0), gp, zs.pop(n - 1), *pcs.pop(n - 1), mp, bias)
    jp, gp, mp = units[-1]
    weights(jp, gp, zs.pop(len(units) - 1), *pcs.pop(len(units) - 1), mp)

    for g in range(sub):
        def live(state):
            j, top = state
            return jnp.logical_and(j >= 0, top >= SB_UNDERFLOW_LOG2)

        def step(state, g=g):
            j, _ = state
            zts = logits(j, g)
            weights(j, g, zts, *log_not(zts, False), False)
            return j - 1, jnp.max(carry_ref[:, :, cols(g)])

        lax.while_loop(live, step, (first + g - 2, jnp.max(carry_ref[:, :, cols(g)])))

    row = lax.broadcasted_iota(jnp.int32, (LANES, tq), 0)
    ot = jnp.where(row < SB_HEAD_DIM, acc_ref[0], acc_ref[1])
    o_ref[0] = ot.T.astype(BF16)


def _sb_attention(q, k, v, mt):
    b, s, _ = q.shape
    tq = SB_TQ
    qspec = pl.BlockSpec((1, tq, LANES), lambda bi, hp, i: (bi, i, hp))
    kvspec = pl.BlockSpec((1, s, LANES), lambda bi, hp, i: (bi, 0, hp))
    return pl.pallas_call(
        _sb_kernel,
        grid=(b, SB_WIDTH // LANES, s // tq),
        in_specs=[qspec, kvspec, kvspec, _const_spec(mt.shape)],
        out_specs=qspec,
        out_shape=jax.ShapeDtypeStruct((b, s, SB_WIDTH), BF16),
        scratch_shapes=[pltpu.VMEM((2, LANES, tq), F32), pltpu.VMEM((2, 1, tq), F32)],
        compiler_params=pltpu.CompilerParams(
            dimension_semantics=("arbitrary", "arbitrary", "arbitrary"),
            vmem_limit_bytes=VMEM_LIMIT_BYTES),
        name="stick_breaking_attention",
    )(q, k, v, mt)


def _gla_level_matrix():
    c = GLA_CHUNK
    tril = np.tril(np.ones((c, c), np.float32))
    mats = [tril]
    rows = np.arange(c)
    for n in GLA_LEVELS:
        base = (rows // n) * n
        second = (rows // n) % 2 == 1
        idx = np.where(second, base - 1, base + n - 1)
        mats.append(tril[idx])
    return np.concatenate(mats, axis=0)


def _gla_kernel(q_ref, k_ref, v_ref, la_ref, r_ref, cm_ref, ng_ref, o_ref, state_ref):
    c = GLA_CHUNK
    nch = GLA_STEP // c

    @pl.when(pl.program_id(2) == 0)
    def _():
        state_ref[...] = jnp.zeros_like(state_ref)

    cm = cm_ref[...]
    row128 = lax.broadcasted_iota(jnp.int32, (c, GLA_DK), 0)
    ri = lax.broadcasted_iota(jnp.int32, (c, c), 0)
    ci = lax.broadcasted_iota(jnp.int32, (c, c), 1)
    nt = (((1,), (1,)), ((), ()))
    tn = (((0,), (0,)), ((), ()))
    sls = [slice(ch * c, (ch + 1) * c) for ch in range(nch)]

    sts = []
    for sl in sls:
        g = la_ref[0, sl, :]
        ghi = g.astype(BF16)
        glo = (g - ghi.astype(F32)).astype(BF16)
        sts.append(jnp.dot(cm, ghi, preferred_element_type=F32)
                   + jnp.dot(cm, glo, preferred_element_type=F32))

    o_intra, q_in, decays, upds = [], [], [], []
    for sl, st in zip(sls, sts):
        cum = st[0:c]
        last = cum[c - 1:c, :]
        qb = q_ref[0, sl, :]
        kb = k_ref[0, sl, :]
        vb = v_ref[0, sl, :]
        qf = qb.astype(F32)
        kf = kb.astype(F32)
        sc = jnp.where(ri == ci, lax.dot_general(qb, kb, nt, preferred_element_type=F32), 0.0)
        for li, n in enumerate(GLA_LEVELS):
            sh = n.bit_length() - 1
            d = cum - st[(li + 1) * c:(li + 2) * c]
            second = ((row128 >> sh) & 1) == 1
            ex = jnp.exp(jnp.where(second, d, -d))
            ql = jnp.where(second, qf * ex, 0.0).astype(BF16)
            kl = jnp.where(second, 0.0, kf * ex).astype(BF16)
            s_l = lax.dot_general(ql, kl, nt, preferred_element_type=F32)
            sc = sc + jnp.where((ri >> (sh + 1)) == (ci >> (sh + 1)), s_l, 0.0)
        o_intra.append(jnp.dot(sc.astype(BF16), vb, preferred_element_type=F32))
        q_in.append((qf * jnp.exp(cum)).astype(BF16))
        kd = (kf * jnp.exp(last - cum)).astype(BF16)
        upds.append(lax.dot_general(kd, vb, tn, preferred_element_type=F32))
        decay = jnp.broadcast_to(jnp.exp(last), (GLA_DK, GLA_DK)).T
        decays.append(jnp.concatenate([decay, decay], axis=1))

    state = state_ref[...]
    for ch, sl in enumerate(sls):
        o = o_intra[ch] + jnp.dot(q_in[ch], state.astype(BF16), preferred_element_type=F32)
        state = state * decays[ch] + upds[ch]
        ms = jnp.mean(o * o, axis=-1, keepdims=True)
        y = o * lax.rsqrt(ms + RMS_EPS) * ng_ref[...]
        o_ref[0, sl, :] = (y * r_ref[0, sl, :].astype(F32)).astype(BF16)
    state_ref[...] = state


def _gla(gq, gk, gv, la, sr, cm, ng):
    b, s, _ = gq.shape
    ts = GLA_STEP
    kspec = pl.BlockSpec((1, ts, GLA_DK), lambda bi, h, t: (bi, t, h))
    vspec = pl.BlockSpec((1, ts, GLA_DV), lambda bi, h, t: (bi, t, h))
    return pl.pallas_call(
        _gla_kernel,
        grid=(b, GLA_HEADS, s // ts),
        in_specs=[kspec, kspec, vspec, kspec, vspec, _const_spec(cm.shape), _const_spec(ng.shape)],
        out_specs=vspec,
        out_shape=jax.ShapeDtypeStruct((b, s, GLA_VALUE_WIDTH), BF16),
        scratch_shapes=[pltpu.VMEM((GLA_DK, GLA_DV), F32)],
        compiler_params=pltpu.CompilerParams(
            dimension_semantics=("arbitrary", "arbitrary", "arbitrary"),
            vmem_limit_bytes=VMEM_LIMIT_BYTES),
        name="gated_linear_attention",
    )(gq, gk, gv, la, sr, cm, ng)


def _tail_kernel(x_ref, osb_ref, ogla_ref, gs_ref, gg_ref, wsb_ref, wgla_ref, wout_ref,
                 g2_ref, wg_ref, wu_ref, wd_ref, o_ref):
    a = jnp.dot(osb_ref[...], wsb_ref[...], preferred_element_type=F32)
    b = jnp.dot(ogla_ref[...], wgla_ref[...], preferred_element_type=F32)
    merged = gs_ref[...].astype(F32) * a + gg_ref[...].astype(F32) * b
    x1 = x_ref[...] + jnp.dot(merged.astype(BF16), wout_ref[...], preferred_element_type=F32)
    ms = jnp.mean(x1 * x1, axis=-1, keepdims=True)
    h2 = (x1 * lax.rsqrt(ms + RMS_EPS) * g2_ref[...]).astype(BF16)
    gate = jnp.dot(h2, wg_ref[...], preferred_element_type=F32)
    up = jnp.dot(h2, wu_ref[...], preferred_element_type=F32)
    hid = (gate * jax.nn.sigmoid(gate) * up).astype(BF16)
    o_ref[...] = x1 + jnp.dot(hid, wd_ref[...], preferred_element_type=F32)


def _tail(x2, osb, ogla, gs, gg, wsb, wgla, wout, g2, wg, wu, wd):
    t = x2.shape[0]
    tm = ROW_TILE
    row = pl.BlockSpec((tm, D_MODEL), lambda i: (i, 0))
    consts = (wsb, wgla, wout, g2, wg, wu, wd)
    return pl.pallas_call(
        _tail_kernel,
        grid=(t // tm,),
        in_specs=[row] * 5 + [_const_spec(c.shape) for c in consts],
        out_specs=row,
        out_shape=jax.ShapeDtypeStruct((t, D_MODEL), F32),
        compiler_params=pltpu.CompilerParams(dimension_semantics=("arbitrary",),
                                             vmem_limit_bytes=VMEM_LIMIT_BYTES),
        name="merge_out_ffn",
    )(x2, osb, ogla, gs, gg, *consts)


def _layer(x, norm1_g, w_in, sb_q_norm_g, sb_k_norm_g, gla_gate_w2, gla_gate_b, gla_out_norm_g,
           w_branch_sb, w_branch_gla, w_out, norm2_g, w_ffn_gate, w_ffn_up, w_ffn_down):
    b, s, d = x.shape
    t = b * s
    x2 = x.reshape(t, d)

    lr0 = 3 * SB_WIDTH + 2 * GLA_KEY_WIDTH + 2 * GLA_VALUE_WIDTH
    lr1 = lr0 + GLA_GATE_RANK
    wm = jnp.concatenate([w_in[:, :lr0], w_in[:, lr1:]], axis=1).astype(BF16)
    wlr = jnp.pad(w_in[:, lr0:lr1], ((0, 0), (0, LANES - GLA_GATE_RANK))).astype(BF16)
    w2p = jnp.pad(gla_gate_w2, ((0, LANES - GLA_GATE_RANK), (0, 0))).astype(BF16)

    heads_per_tile = MXU_DIM // SB_HEAD_DIM
    qg = jnp.tile(sb_q_norm_g * (SB_HEAD_DIM ** -0.5 * LOG2E), heads_per_tile).reshape(1, MXU_DIM)
    kg = jnp.tile(sb_k_norm_g, heads_per_tile).reshape(1, MXU_DIM)
    hid = np.arange(MXU_DIM) // SB_HEAD_DIM
    avg = jnp.asarray((hid[:, None] == hid[None, :]).astype(np.float32) / SB_HEAD_DIM, BF16)

    q, k, v, gq, gk, gv, sr, la, gs, gg = _in_projection(
        x2, norm1_g.reshape(1, d), wm, wlr, w2p, gla_gate_b.reshape(1, -1), qg, kg, avg)

    idx = np.arange(SB_TK)
    mt = jnp.asarray(-(idx[None, :] > idx[:, None]).astype(np.float32), BF16)
    o_sb = _sb_attention(q.reshape(b, s, -1), k.reshape(b, s, -1), v.reshape(b, s, -1), mt)

    cm = jnp.asarray(_gla_level_matrix(), BF16)
    o_gla = _gla(gq.reshape(b, s, -1), gk.reshape(b, s, -1), gv.reshape(b, s, -1),
                 la.reshape(b, s, -1), sr.reshape(b, s, -1), cm, gla_out_norm_g.reshape(1, -1))

    out = _tail(x2, o_sb.reshape(t, -1), o_gla.reshape(t, -1), gs, gg,
                w_branch_sb.astype(BF16), w_branch_gla.astype(BF16), w_out.astype(BF16),
                norm2_g.reshape(1, d), w_ffn_gate.astype(BF16), w_ffn_up.astype(BF16),
                w_ffn_down.astype(BF16))
    return out.reshape(b, s, d)


def kernel(x, norm1_g, w_in, sb_q_norm_g, sb_k_norm_g, gla_gate_w2, gla_gate_b, gla_out_norm_g,
           w_branch_sb, w_branch_gla, w_out, norm2_g, w_ffn_gate, w_ffn_up, w_ffn_down):
    for l in range(norm1_g.shape[0]):
        x = _layer(x, norm1_g[l], w_in[l], sb_q_norm_g[l], sb_k_norm_g[l], gla_gate_w2[l],
                   gla_gate_b[l], gla_out_norm_g[l], w_branch_sb[l], w_branch_gla[l], w_out[l],
                   norm2_g[l], w_ffn_gate[l], w_ffn_up[l], w_ffn_down[l])
    return x
```

```python
import functools

import jax
import jax.numpy as jnp
import numpy as np
from jax import lax
from jax.experimental import pallas as pl
from jax.experimental.pallas import tpu as pltpu

F32 = jnp.float32
BF16 = jnp.bfloat16

D_MODEL = 1024
SB_HEADS = 16
SB_HEAD_DIM = 64
SB_WIDTH = SB_HEADS * SB_HEAD_DIM
GLA_HEADS = 4
GLA_KEY_WIDTH = D_MODEL // 2
GLA_VALUE_WIDTH = D_MODEL
GLA_DK = GLA_KEY_WIDTH // GLA_HEADS
GLA_DV = GLA_VALUE_WIDTH // GLA_HEADS
GLA_GATE_RANK = 16
GLA_GATE_TAU = 16.0
D_FF = -(-8 * D_MODEL // (3 * 256)) * 256
RMS_EPS = 1e-6
LOG2E = 1.4426950408889634

LANES = 128
MXU_DIM = 256
VMEM_LIMIT_BYTES = 56 * 1024 * 1024

ROW_TILE = 256
INPROJ_ROW_TILE = 512
SB_TQ = 1024
SB_TK = 256
SB_QT = 128
SB_UNDERFLOW_LOG2 = -126.0
SB_NO_BLOCK_BIAS = -1e30
GLA_CHUNK = 64
GLA_STEP = 512
GLA_LEVELS = (32, 16, 8, 4, 2, 1)


def _const_spec(shape):
    nd = len(shape)
    return pl.BlockSpec(shape, lambda *_: (0,) * nd, pipeline_mode=pl.Buffered(1))


def _log_sigmoid(x):
    return jnp.minimum(x, 0.0) - jnp.log(1.0 + jnp.exp(-jnp.abs(x)))


def _inproj_kernel(x_ref, g1_ref, wa_ref, wb_ref, wlr_ref, w2_ref, gb_ref, qg_ref, kg_ref, avg_ref,
                   q_ref, k_ref, v_ref, gq_ref, gk_ref, gv_ref, r_ref, la_ref, gs_ref, gg_ref):
    x = x_ref[...]
    ms = jnp.mean(x * x, axis=-1, keepdims=True)
    h = (x * lax.rsqrt(ms + RMS_EPS) * g1_ref[...]).astype(BF16)
    wa_cols = wa_ref.shape[1]

    def proj(c0, cw):
        w = wa_ref[:, c0:c0 + cw] if c0 < wa_cols else wb_ref[:, c0 - wa_cols:c0 - wa_cols + cw]
        return jnp.dot(h, w, preferred_element_type=F32)

    def head_norm(y, gain):
        m = jnp.dot((y * y).astype(BF16), avg_ref[...], preferred_element_type=F32)
        return y * lax.rsqrt(m + RMS_EPS) * gain

    segments = (
        (q_ref, SB_WIDTH, MXU_DIM, lambda y: head_norm(y, qg_ref[...])),
        (k_ref, SB_WIDTH, MXU_DIM, lambda y: head_norm(y, kg_ref[...])),
        (v_ref, SB_WIDTH, 512, lambda y: y),
        (gq_ref, GLA_KEY_WIDTH, 512, lambda y: y * (GLA_DK ** -0.5)),
        (gk_ref, GLA_KEY_WIDTH, 512, lambda y: y),
        (gv_ref, GLA_VALUE_WIDTH, 512, lambda y: y),
        (r_ref, GLA_VALUE_WIDTH, 512, lambda y: y * jax.nn.sigmoid(y)),
        (gs_ref, D_MODEL, 512, jax.nn.sigmoid),
        (gg_ref, D_MODEL, 512, jax.nn.sigmoid),
    )
    jobs, off = [], 0
    for ref, width, cw, post in segments:
        jobs += [(ref, lo, off + lo, cw, post) for lo in range(0, width, cw)]
        off += width

    lr = jnp.dot(h, wlr_ref[...], preferred_element_type=F32)
    y_next = proj(jobs[0][2], jobs[0][3])
    for n, (ref, lo, _, cw, post) in enumerate(jobs):
        y = y_next
        if n + 1 < len(jobs):
            y_next = proj(jobs[n + 1][2], jobs[n + 1][3])
        if n == 0:
            logit = jnp.dot(lr.astype(BF16), w2_ref[...], preferred_element_type=F32) + gb_ref[...]
        ref[:, lo:lo + cw] = post(y).astype(BF16)
    la_ref[...] = _log_sigmoid(logit) * (1.0 / GLA_GATE_TAU)


def _in_projection(x2, g1, wa, wb, wlr, w2p, gb, qg, kg, avg):
    t = x2.shape[0]
    tm = INPROJ_ROW_TILE
    row = lambda width: pl.BlockSpec((tm, width), lambda i: (i, 0))
    bf = lambda width: jax.ShapeDtypeStruct((t, width), BF16)
    out_shape = (bf(SB_WIDTH), bf(SB_WIDTH), bf(SB_WIDTH),
                 bf(GLA_KEY_WIDTH), bf(GLA_KEY_WIDTH), bf(GLA_VALUE_WIDTH), bf(GLA_VALUE_WIDTH),
                 jax.ShapeDtypeStruct((t, GLA_KEY_WIDTH), F32), bf(D_MODEL), bf(D_MODEL))
    out_specs = (row(SB_WIDTH), row(SB_WIDTH), row(SB_WIDTH),
                 row(GLA_KEY_WIDTH), row(GLA_KEY_WIDTH), row(GLA_VALUE_WIDTH), row(GLA_VALUE_WIDTH),
                 row(GLA_KEY_WIDTH), row(D_MODEL), row(D_MODEL))
    return pl.pallas_call(
        _inproj_kernel,
        grid=(t // tm,),
        in_specs=[row(D_MODEL)] + [_const_spec(c.shape) for c in (g1, wa, wb, wlr, w2p, gb, qg, kg, avg)],
        out_specs=out_specs,
        out_shape=out_shape,
        compiler_params=pltpu.CompilerParams(dimension_semantics=("arbitrary",),
                                             vmem_limit_bytes=VMEM_LIMIT_BYTES),
        name="in_projection",
    )(x2, g1, wa, wb, wlr, w2p, gb, qg, kg, avg)


def _sb_kernel(q_ref, k_ref, v_ref, mt_ref, o_ref, acc_ref, carry_ref):
    qi = pl.program_id(2)
    tq, tk, qt = SB_TQ, SB_TK, SB_QT
    sub = tq // tk
    nt = (((1,), (1,)), ((), ()))
    tn = (((0,), (0,)), ((), ()))
    mt = mt_ref[...]
    one = jnp.ones((), BF16)
    zero16 = jnp.zeros((), BF16)
    first = qi * sub

    def head_split(q):
        lane = lax.broadcasted_iota(jnp.int32, q.shape, 1)
        zero = jnp.zeros_like(q)
        return (jnp.where(lane < SB_HEAD_DIM, q, zero), jnp.where(lane >= SB_HEAD_DIM, q, zero))

    def key_block(ref, j):
        return ref[0, pl.ds(pl.multiple_of(j * tk, tk), tk), :]

    def softplus2(zt):
        l2 = jnp.log2((one + jnp.exp2(-jnp.abs(zt))).astype(F32)).astype(BF16)
        return jnp.maximum(zt, zero16) + l2

    def causal(rows, width):
        return (lax.broadcasted_iota(jnp.int32, (rows, width), 0)
                < lax.broadcasted_iota(jnp.int32, (rows, width), 1))

    top_mask = causal(qt, tk)
    br_mask = causal(qt, qt)
    dead = jnp.zeros((qt, qt), BF16)

    def with_dead_quadrant(top, br):
        return jnp.concatenate([top, jnp.concatenate([dead, br], axis=1)], axis=0)

    def diag_logits(g):
        kb = key_block(k_ref, first + g)
        qh = head_split(q_ref[0, g * tk:(g + 1) * tk, :])
        return [lax.dot_general(kb, qh[h], nt, preferred_element_type=F32).astype(BF16)
                for h in range(2)]

    def diag_log_not(zts):
        out = []
        for zt in zts:
            p_top = jnp.where(top_mask, softplus2(zt[:qt, :]), zero16)
            p_br = jnp.where(br_mask, softplus2(zt[qt:, qt:]), zero16)
            cs = jnp.dot(mt, with_dead_quadrant(p_top, p_br), preferred_element_type=F32)
            out.append((p_top, p_br, cs))
        return out

    def diag_weights(g, zts, pcs):
        vb = key_block(v_ref, first + g)
        out = []
        for h in range(2):
            p_top, p_br, cs = pcs[h]
            zt = zts[h]
            w_top = jnp.exp2((zt[:qt, :] - p_top) + cs[:qt, :].astype(BF16))
            w_br = jnp.exp2((zt[qt:, qt:] - p_br) + cs[qt:, qt:].astype(BF16))
            w = with_dead_quadrant(jnp.where(top_mask, w_top, zero16), jnp.where(br_mask, w_br, zero16))
            pv = lax.dot_general(vb, w, tn, preferred_element_type=F32)
            out.append((pv, cs[0:1, :] - p_top[0:1, :].astype(F32)))
        return out

    def tile_logits(j, qh):
        kb = key_block(k_ref, j)
        return lax.dot_general(kb, jnp.concatenate(qh, axis=0), nt,
                               preferred_element_type=F32).astype(BF16)

    def tile_log_not(zt):
        p = softplus2(zt)
        return p, jnp.dot(mt, p, preferred_element_type=F32)

    def tile_weights(j, zt, pc, rows):
        vb = key_block(v_ref, j)
        p, cs = pc
        w = jnp.exp2(((zt - p) + cs.astype(BF16)) + jnp.concatenate(rows, axis=1).astype(BF16))
        pv = lax.dot_general(vb, w, tn, preferred_element_type=F32)
        tot = cs[0:1, :] - p[0:1, :].astype(F32)
        return [(pv[:, h * qt:(h + 1) * qt], tot[:, h * qt:(h + 1) * qt]) for h in range(2)]

    no_prev_bias = jnp.where(first > 0, 0.0, SB_NO_BLOCK_BIAS).astype(F32)
    prev_block = [jnp.maximum(first + g - 1, 0) for g in range(sub)]
    prev_q = [None] * sub
    n_units = 2 * sub
    zs, pcs, diag_out = {}, {}, {}

    def issue_logits(n):
        if n < sub:
            zs[n] = diag_logits(n)
        else:
            g = n - sub
            prev_q[g] = head_split(q_ref[0, g * tk:g * tk + qt, :])
            zs[n] = tile_logits(prev_block[g], prev_q[g])

    def issue_weights(n):
        if n < sub:
            diag_out[n] = diag_weights(n, zs.pop(n), pcs.pop(n))
            return
        g = n - sub
        rows = [diag_out[g][h][1][:, :qt] for h in range(2)]
        if g == 0:
            rows = [r + no_prev_bias for r in rows]
        res = tile_weights(prev_block[g], zs.pop(n), pcs.pop(n), rows)
        for h in range(2):
            pv_d, tot_d = diag_out[g][h]
            pv_p, tot_p = res[h]
            acc_ref[h, 2 * g] = pv_d[:, :qt] + pv_p
            acc_ref[h, 2 * g + 1] = pv_d[:, qt:]
            carry_ref[h, 2 * g] = tot_d[:, :qt] + tot_p
            carry_ref[h, 2 * g + 1] = tot_d[:, qt:]

    issue_logits(0)
    for n in range(n_units):
        if n + 1 < n_units:
            issue_logits(n + 1)
        pcs[n] = diag_log_not(zs[n]) if n < sub else tile_log_not(zs[n])
        if n >= 1:
            issue_weights(n - 1)
    issue_weights(n_units - 1)

    @pl.when(jnp.max(carry_ref[...]) >= SB_UNDERFLOW_LOG2)
    def _():
        def tile_body(c, _):
            qh = head_split(q_ref[0, pl.ds(pl.multiple_of(c * qt, qt), qt), :])

            def live(state):
                j, top = state
                return jnp.logical_and(j >= 0, top >= SB_UNDERFLOW_LOG2)

            def step(state):
                j, _ = state
                zts = tile_logits(j, qh)
                rows = [carry_ref[h, c] for h in range(2)]
                res = tile_weights(j, zts, tile_log_not(zts), rows)
                for h in range(2):
                    acc_ref[h, c] += res[h][0]
                    carry_ref[h, c] = rows[h] + res[h][1]
                return j - 1, jnp.max(carry_ref[:, c])

            lax.while_loop(live, step, (first + (c >> 1) - 2 + (c & 1), jnp.max(carry_ref[:, c])))
            return 0

        lax.fori_loop(0, tq // qt, tile_body, 0)

    row = lax.broadcasted_iota(jnp.int32, (LANES, qt), 0)
    for c in range(tq // qt):
        ot = jnp.where(row < SB_HEAD_DIM, acc_ref[0, c], acc_ref[1, c])
        o_ref[0, c * qt:(c + 1) * qt, :] = ot.T.astype(BF16)


def _sb_attention(q, k, v, mt):
    b, s, _ = q.shape
    tq = SB_TQ
    qspec = pl.BlockSpec((1, tq, LANES), lambda bi, hp, i: (bi, i, hp))
    kvspec = pl.BlockSpec((1, s, LANES), lambda bi, hp, i: (bi, 0, hp))
    return pl.pallas_call(
        _sb_kernel,
        grid=(b, SB_WIDTH // LANES, s // tq),
        in_specs=[qspec, kvspec, kvspec, _const_spec(mt.shape)],
        out_specs=qspec,
        out_shape=jax.ShapeDtypeStruct((b, s, SB_WIDTH), BF16),
        scratch_shapes=[pltpu.VMEM((2, tq // SB_QT, LANES, SB_QT), F32),
                        pltpu.VMEM((2, tq // SB_QT, 1, SB_QT), F32)],
        compiler_params=pltpu.CompilerParams(
            dimension_semantics=("arbitrary", "arbitrary", "arbitrary"),
            vmem_limit_bytes=VMEM_LIMIT_BYTES),
        name="stick_breaking_attention",
    )(q, k, v, mt)


def _gla_pivot_rows(n):
    rows = np.arange(GLA_CHUNK)
    base = (rows // n) * n
    second = (rows // n) % 2 == 1
    return second, np.where(second, base - 1, base + n - 1)


def _gla_level_matrix():
    c = GLA_CHUNK
    r = np.arange(c)[:, None]
    m = np.arange(c)[None, :]
    mats = [m <= r, m > r]
    for n in GLA_LEVELS:
        _, idx = _gla_pivot_rows(n)
        lo = np.minimum(r[:, 0], idx)[:, None]
        hi = np.maximum(r[:, 0], idx)[:, None]
        mats.append((m > lo) & (m <= hi))
    one = np.concatenate(mats, axis=0).astype(np.float32)
    return np.concatenate([one, one], axis=1)


def _gla_kernel(q_ref, k_ref, v_ref, la_ref, r_ref, cm_ref, ng_ref, o_ref, state_ref):
    c = GLA_CHUNK
    nch = GLA_STEP // c

    @pl.when(pl.program_id(2) == 0)
    def _():
        state_ref[...] = jnp.zeros_like(state_ref)

    cm = cm_ref[...]
    ri = lax.broadcasted_iota(jnp.int32, (c, c), 0)
    ci = lax.broadcasted_iota(jnp.int32, (c, c), 1)
    nt = (((1,), (1,)), ((), ()))
    tn = (((0,), (0,)), ((), ()))
    sls = [slice(ch * c, (ch + 1) * c) for ch in range(nch)]

    level_masks = []
    for n in GLA_LEVELS:
        sh = n.bit_length() - 1
        level_masks.append(((ri >> sh) == (ci >> sh) + 1) & (((ri >> sh) & 1) == 1))

    args = []
    for sl in sls:
        g = la_ref[0, sl, :]
        ghi = g.astype(BF16)
        glo = (g - ghi.astype(F32)).astype(BF16)
        args.append(jnp.dot(cm, jnp.concatenate([ghi, glo], axis=0),
                            preferred_element_type=F32))

    o_intra, q_in, decays, upds = [], [], [], []
    for sl, arg in zip(sls, args):
        ex = jnp.exp(arg.astype(BF16))
        qb = q_ref[0, sl, :]
        kb = k_ref[0, sl, :]
        vb = v_ref[0, sl, :]
        sc = jnp.where(ri == ci, lax.dot_general(qb, kb, nt, preferred_element_type=F32), 0.0)
        for li in range(len(GLA_LEVELS)):
            ex_l = ex[(li + 2) * c:(li + 3) * c]
            s_l = lax.dot_general(qb * ex_l, kb * ex_l, nt, preferred_element_type=F32)
            sc = jnp.where(level_masks[li], s_l, sc)
        o_intra.append(jnp.dot(sc.astype(BF16), vb, preferred_element_type=F32))
        q_in.append(qb * ex[0:c])
        upds.append(lax.dot_general(kb * ex[c:2 * c], vb, tn, preferred_element_type=F32))
        last = arg[c - 1:c, :]
        decay = jnp.broadcast_to(jnp.exp(last), (GLA_DK, GLA_DK)).T
        decays.append(jnp.concatenate([decay, decay], axis=1))

    state = state_ref[...]
    for ch, sl in enumerate(sls):
        o = o_intra[ch] + jnp.dot(q_in[ch], state.astype(BF16), preferred_element_type=F32)
        state = state * decays[ch] + upds[ch]
        ms = jnp.mean(o * o, axis=-1, keepdims=True)
        y = o * lax.rsqrt(ms + RMS_EPS) * ng_ref[...]
        o_ref[0, sl, :] = (y * r_ref[0, sl, :].astype(F32)).astype(BF16)
    state_ref[...] = state


def _gla(gq, gk, gv, la, sr, cm, ng):
    b, s, _ = gq.shape
    ts = GLA_STEP
    kspec = pl.BlockSpec((1, ts, GLA_DK), lambda bi, h, t: (bi, t, h))
    vspec = pl.BlockSpec((1, ts, GLA_DV), lambda bi, h, t: (bi, t, h))
    return pl.pallas_call(
        _gla_kernel,
        grid=(b, GLA_HEADS, s // ts),
        in_specs=[kspec, kspec, vspec, kspec, vspec, _const_spec(cm.shape), _const_spec(ng.shape)],
        out_specs=vspec,
        out_shape=jax.ShapeDtypeStruct((b, s, GLA_VALUE_WIDTH), BF16),
        scratch_shapes=[pltpu.VMEM((GLA_DK, GLA_DV), F32)],
        compiler_params=pltpu.CompilerParams(
            dimension_semantics=("arbitrary", "arbitrary", "arbitrary"),
            vmem_limit_bytes=VMEM_LIMIT_BYTES),
        name="gated_linear_attention",
    )(gq, gk, gv, la, sr, cm, ng)


def _tail_kernel(x_ref, osb_ref, ogla_ref, gs_ref, gg_ref, wsb_ref, wgla_ref, wout_ref,
                 g2_ref, wg_ref, wu_ref, wd_ref, o_ref):
    a = jnp.dot(osb_ref[...], wsb_ref[...], preferred_element_type=F32)
    b = jnp.dot(ogla_ref[...], wgla_ref[...], preferred_element_type=F32)
    merged = gs_ref[...].astype(F32) * a + gg_ref[...].astype(F32) * b
    x1 = x_ref[...] + jnp.dot(merged.astype(BF16), wout_ref[...], preferred_element_type=F32)
    ms = jnp.mean(x1 * x1, axis=-1, keepdims=True)
    h2 = (x1 * lax.rsqrt(ms + RMS_EPS) * g2_ref[...]).astype(BF16)
    gate = jnp.dot(h2, wg_ref[...], preferred_element_type=F32)
    up = jnp.dot(h2, wu_ref[...], preferred_element_type=F32)
    hid = (gate * jax.nn.sigmoid(gate) * up).astype(BF16)
    o_ref[...] = x1 + jnp.dot(hid, wd_ref[...], preferred_element_type=F32)


def _tail(x2, osb, ogla, gs, gg, wsb, wgla, wout, g2, wg, wu, wd):
    t = x2.shape[0]
    tm = ROW_TILE
    row = pl.BlockSpec((tm, D_MODEL), lambda i: (i, 0))
    consts = (wsb, wgla, wout, g2, wg, wu, wd)
    return pl.pallas_call(
        _tail_kernel,
        grid=(t // tm,),
        in_specs=[row] * 5 + [_const_spec(c.shape) for c in consts],
        out_specs=row,
        out_shape=jax.ShapeDtypeStruct((t, D_MODEL), F32),
        compiler_params=pltpu.CompilerParams(dimension_semantics=("arbitrary",),
                                             vmem_limit_bytes=VMEM_LIMIT_BYTES),
        name="merge_out_ffn",
    )(x2, osb, ogla, gs, gg, *consts)


def _layer(x, norm1_g, w_in, sb_q_norm_g, sb_k_norm_g, gla_gate_w2, gla_gate_b, gla_out_norm_g,
           w_branch_sb, w_branch_gla, w_out, norm2_g, w_ffn_gate, w_ffn_up, w_ffn_down):
    b, s, d = x.shape
    t = b * s
    x2 = x.reshape(t, d)

    lr0 = 3 * SB_WIDTH + 2 * GLA_KEY_WIDTH + 2 * GLA_VALUE_WIDTH
    lr1 = lr0 + GLA_GATE_RANK
    wa = w_in[:, :lr0].astype(BF16)
    wb = w_in[:, lr1:].astype(BF16)
    wlr = jnp.pad(w_in[:, lr0:lr1], ((0, 0), (0, LANES - GLA_GATE_RANK))).astype(BF16)
    w2p = jnp.pad(gla_gate_w2, ((0, LANES - GLA_GATE_RANK), (0, 0))).astype(BF16)

    heads_per_tile = MXU_DIM // SB_HEAD_DIM
    qg = jnp.tile(sb_q_norm_g * (SB_HEAD_DIM ** -0.5 * LOG2E), heads_per_tile).reshape(1, MXU_DIM)
    kg = jnp.tile(sb_k_norm_g, heads_per_tile).reshape(1, MXU_DIM)
    hid = np.arange(MXU_DIM) // SB_HEAD_DIM
    avg = jnp.asarray((hid[:, None] == hid[None, :]).astype(np.float32) / SB_HEAD_DIM, BF16)

    q, k, v, gq, gk, gv, sr, la, gs, gg = _in_projection(
        x2, norm1_g.reshape(1, d), wa, wb, wlr, w2p, gla_gate_b.reshape(1, -1), qg, kg, avg)

    idx = np.arange(SB_TK)
    mt = jnp.asarray(-(idx[None, :] > idx[:, None]).astype(np.float32), BF16)
    o_sb = _sb_attention(q.reshape(b, s, -1), k.reshape(b, s, -1), v.reshape(b, s, -1), mt)

    cm = jnp.asarray(_gla_level_matrix(), BF16)
    o_gla = _gla(gq.reshape(b, s, -1), gk.reshape(b, s, -1), gv.reshape(b, s, -1),
                 la.reshape(b, s, -1), sr.reshape(b, s, -1), cm, gla_out_norm_g.reshape(1, -1))

    out = _tail(x2, o_sb.reshape(t, -1), o_gla.reshape(t, -1), gs, gg,
                w_branch_sb.astype(BF16), w_branch_gla.astype(BF16), w_out.astype(BF16),
                norm2_g.reshape(1, d), w_ffn_gate.astype(BF16), w_ffn_up.astype(BF16),
                w_ffn_down.astype(BF16))
    return out.reshape(b, s, d)


def kernel(x, norm1_g, w_in, sb_q_norm_g, sb_k_norm_g, gla_gate_w2, gla_gate_b, gla_out_norm_g,
           w_branch_sb, w_branch_gla, w_out, norm2_g, w_ffn_gate, w_ffn_up, w_ffn_down):
    for l in range(norm1_g.shape[0]):
        x = _layer(x, norm1_g[l], w_in[l], sb_q_norm_g[l], sb_k_norm_g[l], gla_gate_w2[l],
                   gla_gate_b[l], gla_out_norm_g[l], w_branch_sb[l], w_branch_gla[l], w_out[l],
                   norm2_g[l], w_ffn_gate[l], w_ffn_up[l], w_ffn_down[l])
    return x
```

```python
import functools

import jax
import jax.numpy as jnp
import numpy as np
from jax import lax
from jax.experimental import pallas as pl
from jax.experimental.pallas import tpu as pltpu

F32 = jnp.float32
BF16 = jnp.bfloat16

D_MODEL = 1024
SB_HEADS = 16
SB_HEAD_DIM = 64
SB_WIDTH = SB_HEADS * SB_HEAD_DIM
GLA_HEADS = 4
GLA_KEY_WIDTH = D_MODEL // 2
GLA_VALUE_WIDTH = D_MODEL
GLA_DK = GLA_KEY_WIDTH // GLA_HEADS
GLA_DV = GLA_VALUE_WIDTH // GLA_HEADS
GLA_GATE_RANK = 16
GLA_GATE_TAU = 16.0
D_FF = -(-8 * D_MODEL // (3 * 256)) * 256
RMS_EPS = 1e-6
LOG2E = 1.4426950408889634

LANES = 128
MXU_DIM = 256
VMEM_LIMIT_BYTES = 56 * 1024 * 1024

ROW_TILE = 512
INPROJ_ROW_TILE = 512
SB_TQ = 4096
SB_TK = 256
SB_QT = 128
SB_UNDERFLOW_LOG2 = -126.0
SB_NO_BLOCK_BIAS = -1e30
GLA_CHUNK = 64
GLA_STEP = 512
GLA_LEVELS = (32, 16, 8, 4, 2, 1)


def _const_spec(shape):
    nd = len(shape)
    return pl.BlockSpec(shape, lambda *_: (0,) * nd, pipeline_mode=pl.Buffered(1))


def _log_sigmoid(x):
    return jnp.minimum(x, 0.0) - jnp.log(1.0 + jnp.exp(-jnp.abs(x)))


def _inproj_kernel(x_ref, g1_ref, wa_ref, wb_ref, wlr_ref, w2_ref, gb_ref, qg_ref, kg_ref, avg_ref,
                   q_ref, k_ref, v_ref, gq_ref, gk_ref, gv_ref, r_ref, la_ref, gs_ref, gg_ref):
    x = x_ref[...]
    ms = jnp.mean(x * x, axis=-1, keepdims=True)
    h = (x * lax.rsqrt(ms + RMS_EPS) * g1_ref[...]).astype(BF16)
    wa_cols = wa_ref.shape[1]

    def proj(c0, cw):
        w = wa_ref[:, c0:c0 + cw] if c0 < wa_cols else wb_ref[:, c0 - wa_cols:c0 - wa_cols + cw]
        return jnp.dot(h, w, preferred_element_type=F32)

    def head_norm(y, gain):
        m = jnp.dot((y * y).astype(BF16), avg_ref[...], preferred_element_type=F32)
        return y * lax.rsqrt(m + RMS_EPS) * gain

    segments = (
        (q_ref, SB_WIDTH, MXU_DIM, lambda y: head_norm(y, qg_ref[...])),
        (k_ref, SB_WIDTH, MXU_DIM, lambda y: head_norm(y, kg_ref[...])),
        (v_ref, SB_WIDTH, 512, lambda y: y),
        (gq_ref, GLA_KEY_WIDTH, 512, lambda y: y * (GLA_DK ** -0.5)),
        (gk_ref, GLA_KEY_WIDTH, 512, lambda y: y),
        (gv_ref, GLA_VALUE_WIDTH, 512, lambda y: y),
        (r_ref, GLA_VALUE_WIDTH, 512, lambda y: y * jax.nn.sigmoid(y)),
        (gs_ref, D_MODEL, 512, jax.nn.sigmoid),
        (gg_ref, D_MODEL, 512, jax.nn.sigmoid),
    )
    jobs, off = [], 0
    for ref, width, cw, post in segments:
        jobs += [(ref, lo, off + lo, cw, post) for lo in range(0, width, cw)]
        off += width

    lr = jnp.dot(h, wlr_ref[...], preferred_element_type=F32)
    y_next = proj(jobs[0][2], jobs[0][3])
    for n, (ref, lo, _, cw, post) in enumerate(jobs):
        y = y_next
        if n + 1 < len(jobs):
            y_next = proj(jobs[n + 1][2], jobs[n + 1][3])
        if n == 0:
            logit = jnp.dot(lr.astype(BF16), w2_ref[...], preferred_element_type=F32) + gb_ref[...]
        ref[:, lo:lo + cw] = post(y).astype(BF16)
    la_ref[...] = _log_sigmoid(logit) * (1.0 / GLA_GATE_TAU)


def _in_projection(x2, g1, wa, wb, wlr, w2p, gb, qg, kg, avg):
    t = x2.shape[0]
    tm = INPROJ_ROW_TILE
    row = lambda width: pl.BlockSpec((tm, width), lambda i: (i, 0))
    bf = lambda width: jax.ShapeDtypeStruct((t, width), BF16)
    out_shape = (bf(SB_WIDTH), bf(SB_WIDTH), bf(SB_WIDTH),
                 bf(GLA_KEY_WIDTH), bf(GLA_KEY_WIDTH), bf(GLA_VALUE_WIDTH), bf(GLA_VALUE_WIDTH),
                 jax.ShapeDtypeStruct((t, GLA_KEY_WIDTH), F32), bf(D_MODEL), bf(D_MODEL))
    out_specs = (row(SB_WIDTH), row(SB_WIDTH), row(SB_WIDTH),
                 row(GLA_KEY_WIDTH), row(GLA_KEY_WIDTH), row(GLA_VALUE_WIDTH), row(GLA_VALUE_WIDTH),
                 row(GLA_KEY_WIDTH), row(D_MODEL), row(D_MODEL))
    return pl.pallas_call(
        _inproj_kernel,
        grid=(t // tm,),
        in_specs=[row(D_MODEL)] + [_const_spec(c.shape) for c in (g1, wa, wb, wlr, w2p, gb, qg, kg, avg)],
        out_specs=out_specs,
        out_shape=out_shape,
        compiler_params=pltpu.CompilerParams(dimension_semantics=("arbitrary",),
                                             vmem_limit_bytes=VMEM_LIMIT_BYTES),
        name="in_projection",
    )(x2, g1, wa, wb, wlr, w2p, gb, qg, kg, avg)


def _sb_kernel(q_ref, k_ref, v_ref, mt_ref, o_ref, acc_ref, carry_ref):
    qi = pl.program_id(2)
    tq, tk, qt = q_ref.shape[1], SB_TK, SB_QT
    sub = tq // tk
    nt = (((1,), (1,)), ((), ()))
    tn = (((0,), (0,)), ((), ()))
    mt = mt_ref[...]
    one = jnp.ones((), BF16)
    zero16 = jnp.zeros((), BF16)
    first = qi * sub

    def head_split(q):
        lane = lax.broadcasted_iota(jnp.int32, q.shape, 1)
        zero = jnp.zeros_like(q)
        return (jnp.where(lane < SB_HEAD_DIM, q, zero), jnp.where(lane >= SB_HEAD_DIM, q, zero))

    def key_block(ref, j):
        return ref[0, pl.ds(pl.multiple_of(j * tk, tk), tk), :]

    def softplus2(zt):
        l2 = jnp.log2((one + jnp.exp2(-jnp.abs(zt))).astype(F32)).astype(BF16)
        return jnp.maximum(zt, zero16) + l2

    def causal(rows, width):
        return (lax.broadcasted_iota(jnp.int32, (rows, width), 0)
                < lax.broadcasted_iota(jnp.int32, (rows, width), 1))

    top_mask = causal(qt, tk)
    br_mask = causal(qt, qt)
    dead = jnp.zeros((qt, qt), BF16)

    def with_dead_quadrant(top, br):
        return jnp.concatenate([top, jnp.concatenate([dead, br], axis=1)], axis=0)

    def diag_logits(g):
        kb = key_block(k_ref, first + g)
        qh = head_split(q_ref[0, g * tk:(g + 1) * tk, :])
        return [lax.dot_general(kb, qh[h], nt, preferred_element_type=F32).astype(BF16)
                for h in range(2)]

    def diag_log_not(zts):
        out = []
        for zt in zts:
            p_top = jnp.where(top_mask, softplus2(zt[:qt, :]), zero16)
            p_br = jnp.where(br_mask, softplus2(zt[qt:, qt:]), zero16)
            cs = jnp.dot(mt, with_dead_quadrant(p_top, p_br), preferred_element_type=F32)
            out.append((p_top, p_br, cs))
        return out

    def diag_weights(g, zts, pcs):
        vb = key_block(v_ref, first + g)
        out = []
        for h in range(2):
            p_top, p_br, cs = pcs[h]
            zt = zts[h]
            w_top = jnp.exp2((zt[:qt, :] - p_top) + cs[:qt, :].astype(BF16))
            w_br = jnp.exp2((zt[qt:, qt:] - p_br) + cs[qt:, qt:].astype(BF16))
            w = with_dead_quadrant(jnp.where(top_mask, w_top, zero16), jnp.where(br_mask, w_br, zero16))
            pv = lax.dot_general(vb, w, tn, preferred_element_type=F32)
            out.append((pv, cs[0:1, :] - p_top[0:1, :].astype(F32)))
        return out

    def tile_logits(j, qh):
        kb = key_block(k_ref, j)
        return lax.dot_general(kb, jnp.concatenate(qh, axis=0), nt,
                               preferred_element_type=F32).astype(BF16)

    def tile_log_not(zt):
        p = softplus2(zt)
        return p, jnp.dot(mt, p, preferred_element_type=F32)

    def tile_weights(j, zt, pc, rows):
        vb = key_block(v_ref, j)
        p, cs = pc
        w = jnp.exp2(((zt - p) + cs.astype(BF16)) + jnp.concatenate(rows, axis=1).astype(BF16))
        pv = lax.dot_general(vb, w, tn, preferred_element_type=F32)
        tot = cs[0:1, :] - p[0:1, :].astype(F32)
        return [(pv[:, h * qt:(h + 1) * qt], tot[:, h * qt:(h + 1) * qt]) for h in range(2)]

    no_prev_bias = jnp.where(first > 0, 0.0, SB_NO_BLOCK_BIAS).astype(F32)
    prev_block = [jnp.maximum(first + g - 1, 0) for g in range(sub)]
    prev_q = [None] * sub
    n_units = 2 * sub
    zs, pcs, diag_out = {}, {}, {}

    def issue_logits(n):
        if n < sub:
            zs[n] = diag_logits(n)
        else:
            g = n - sub
            prev_q[g] = head_split(q_ref[0, g * tk:g * tk + qt, :])
            zs[n] = tile_logits(prev_block[g], prev_q[g])

    def issue_weights(n):
        if n < sub:
            diag_out[n] = diag_weights(n, zs.pop(n), pcs.pop(n))
            return
        g = n - sub
        rows = [diag_out[g][h][1][:, :qt] for h in range(2)]
        if g == 0:
            rows = [r + no_prev_bias for r in rows]
        res = tile_weights(prev_block[g], zs.pop(n), pcs.pop(n), rows)
        for h in range(2):
            pv_d, tot_d = diag_out[g][h]
            pv_p, tot_p = res[h]
            acc_ref[h, 2 * g] = pv_d[:, :qt] + pv_p
            acc_ref[h, 2 * g + 1] = pv_d[:, qt:]
            carry_ref[h, 2 * g] = tot_d[:, :qt] + tot_p
            carry_ref[h, 2 * g + 1] = tot_d[:, qt:]

    issue_logits(0)
    for n in range(n_units):
        if n + 1 < n_units:
            issue_logits(n + 1)
        pcs[n] = diag_log_not(zs[n]) if n < sub else tile_log_not(zs[n])
        if n >= 1:
            issue_weights(n - 1)
    issue_weights(n_units - 1)

    @pl.when(jnp.max(carry_ref[...]) >= SB_UNDERFLOW_LOG2)
    def _():
        def tile_body(c, _):
            qh = head_split(q_ref[0, pl.ds(pl.multiple_of(c * qt, qt), qt), :])

            def live(state):
                j, top = state
                return jnp.logical_and(j >= 0, top >= SB_UNDERFLOW_LOG2)

            def step(state):
                j, _ = state
                zts = tile_logits(j, qh)
                rows = [carry_ref[h, c] for h in range(2)]
                res = tile_weights(j, zts, tile_log_not(zts), rows)
                for h in range(2):
                    acc_ref[h, c] += res[h][0]
                    carry_ref[h, c] = rows[h] + res[h][1]
                return j - 1, jnp.max(carry_ref[:, c])

            lax.while_loop(live, step, (first + (c >> 1) - 2 + (c & 1), jnp.max(carry_ref[:, c])))
            return 0

        lax.fori_loop(0, tq // qt, tile_body, 0)

    row = lax.broadcasted_iota(jnp.int32, (LANES, qt), 0)
    for c in range(tq // qt):
        ot = jnp.where(row < SB_HEAD_DIM, acc_ref[0, c], acc_ref[1, c])
        o_ref[0, c * qt:(c + 1) * qt, :] = ot.T.astype(BF16)


def _sb_attention(q, k, v, mt):
    b, s, _ = q.shape
    tq = min(SB_TQ, s)
    qspec = pl.BlockSpec((1, tq, LANES), lambda bi, hp, i: (bi, i, hp))
    kvspec = pl.BlockSpec((1, s, LANES), lambda bi, hp, i: (bi, 0, hp))
    return pl.pallas_call(
        _sb_kernel,
        grid=(b, SB_WIDTH // LANES, s // tq),
        in_specs=[qspec, kvspec, kvspec, _const_spec(mt.shape)],
        out_specs=qspec,
        out_shape=jax.ShapeDtypeStruct((b, s, SB_WIDTH), BF16),
        scratch_shapes=[pltpu.VMEM((2, tq // SB_QT, LANES, SB_QT), F32),
                        pltpu.VMEM((2, tq // SB_QT, 1, SB_QT), F32)],
        compiler_params=pltpu.CompilerParams(
            dimension_semantics=("arbitrary", "arbitrary", "arbitrary"),
            vmem_limit_bytes=VMEM_LIMIT_BYTES),
        name="stick_breaking_attention",
    )(q, k, v, mt)


def _gla_pivot_rows(n):
    rows = np.arange(GLA_CHUNK)
    base = (rows // n) * n
    second = (rows // n) % 2 == 1
    return second, np.where(second, base - 1, base + n - 1)


def _gla_level_matrix():
    c = GLA_CHUNK
    r = np.arange(c)[:, None]
    m = np.arange(c)[None, :]
    mats = [m <= r, m > r]
    for n in GLA_LEVELS:
        _, idx = _gla_pivot_rows(n)
        lo = np.minimum(r[:, 0], idx)[:, None]
        hi = np.maximum(r[:, 0], idx)[:, None]
        mats.append((m > lo) & (m <= hi))
    one = np.concatenate(mats, axis=0).astype(np.float32)
    return np.concatenate([one, one], axis=1)


def _gla_kernel(q_ref, k_ref, v_ref, la_ref, r_ref, cm_ref, ng_ref, o_ref, state_ref):
    c = GLA_CHUNK
    nch = GLA_STEP // c

    @pl.when(pl.program_id(2) == 0)
    def _():
        state_ref[...] = jnp.zeros_like(state_ref)

    cm = cm_ref[...]
    ri = lax.broadcasted_iota(jnp.int32, (c, c), 0)
    ci = lax.broadcasted_iota(jnp.int32, (c, c), 1)
    nt = (((1,), (1,)), ((), ()))
    tn = (((0,), (0,)), ((), ()))
    sls = [slice(ch * c, (ch + 1) * c) for ch in range(nch)]

    level_masks = []
    for n in GLA_LEVELS:
        sh = n.bit_length() - 1
        level_masks.append(((ri >> sh) == (ci >> sh) + 1) & (((ri >> sh) & 1) == 1))

    def hi_lo(sl):
        g = la_ref[0, sl, :]
        ghi = g.astype(BF16)
        glo = (g - ghi.astype(F32)).astype(BF16)
        return jnp.concatenate([ghi, glo], axis=0)

    args = []
    for ch in range(0, nch, 2):
        pair = jnp.dot(cm, jnp.concatenate([hi_lo(sls[ch]), hi_lo(sls[ch + 1])], axis=1),
                       preferred_element_type=F32)
        args += [pair[:, :GLA_DK], pair[:, GLA_DK:]]

    o_intra, q_in, decays, upds = [], [], [], []
    for sl, arg in zip(sls, args):
        ex = jnp.exp(arg.astype(BF16))
        qb = q_ref[0, sl, :]
        kb = k_ref[0, sl, :]
        vb = v_ref[0, sl, :]
        sc = jnp.where(ri == ci, lax.dot_general(qb, kb, nt, preferred_element_type=F32), 0.0)
        for li in range(len(GLA_LEVELS)):
            ex_l = ex[(li + 2) * c:(li + 3) * c]
            s_l = lax.dot_general(qb * ex_l, kb * ex_l, nt, preferred_element_type=F32)
            sc = jnp.where(level_masks[li], s_l, sc)
        o_intra.append(jnp.dot(sc.astype(BF16), vb, preferred_element_type=F32))
        q_in.append(qb * ex[0:c])
        upds.append(lax.dot_general(kb * ex[c:2 * c], vb, tn, preferred_element_type=F32))
        last = arg[c - 1:c, :]
        decay = jnp.broadcast_to(jnp.exp(last), (GLA_DK, GLA_DK)).T
        decays.append(jnp.concatenate([decay, decay], axis=1))

    state = state_ref[...]
    for ch, sl in enumerate(sls):
        o = o_intra[ch] + jnp.dot(q_in[ch], state.astype(BF16), preferred_element_type=F32)
        state = state * decays[ch] + upds[ch]
        ms = jnp.mean(o * o, axis=-1, keepdims=True)
        y = o * lax.rsqrt(ms + RMS_EPS) * ng_ref[...]
        o_ref[0, sl, :] = (y * r_ref[0, sl, :].astype(F32)).astype(BF16)
    state_ref[...] = state


def _gla(gq, gk, gv, la, sr, cm, ng):
    b, s, _ = gq.shape
    ts = GLA_STEP
    kspec = pl.BlockSpec((1, ts, GLA_DK), lambda bi, h, t: (bi, t, h))
    vspec = pl.BlockSpec((1, ts, GLA_DV), lambda bi, h, t: (bi, t, h))
    return pl.pallas_call(
        _gla_kernel,
        grid=(b, GLA_HEADS, s // ts),
        in_specs=[kspec, kspec, vspec, kspec, vspec, _const_spec(cm.shape), _const_spec(ng.shape)],
        out_specs=vspec,
        out_shape=jax.ShapeDtypeStruct((b, s, GLA_VALUE_WIDTH), BF16),
        scratch_shapes=[pltpu.VMEM((GLA_DK, GLA_DV), F32)],
        compiler_params=pltpu.CompilerParams(
            dimension_semantics=("arbitrary", "arbitrary", "arbitrary"),
            vmem_limit_bytes=VMEM_LIMIT_BYTES),
        name="gated_linear_attention",
    )(gq, gk, gv, la, sr, cm, ng)


def _tail_kernel(x_ref, osb_ref, ogla_ref, gs_ref, gg_ref, wsb_ref, wgla_ref, wout_ref,
                 g2_ref, wg_ref, wu_ref, wd_ref, o_ref):
    tm = x_ref.shape[0]
    halves = [slice(0, tm // 2), slice(tm // 2, tm)]
    dot = functools.partial(jnp.dot, preferred_element_type=F32)
    ab = [(dot(osb_ref[sl, :], wsb_ref[...]), dot(ogla_ref[sl, :], wgla_ref[...])) for sl in halves]
    merged = [(gs_ref[sl, :].astype(F32) * a + gg_ref[sl, :].astype(F32) * b).astype(BF16)
              for sl, (a, b) in zip(halves, ab)]
    x1 = [x_ref[sl, :] + dot(m, wout_ref[...]) for sl, m in zip(halves, merged)]
    h2 = []
    for x1h in x1:
        ms = jnp.mean(x1h * x1h, axis=-1, keepdims=True)
        h2.append((x1h * lax.rsqrt(ms + RMS_EPS) * g2_ref[...]).astype(BF16))
    gu = [(dot(h, wg_ref[...]), dot(h, wu_ref[...])) for h in h2]
    hid = [(gate * jax.nn.sigmoid(gate) * up).astype(BF16) for gate, up in gu]
    for sl, x1h, hd in zip(halves, x1, hid):
        o_ref[sl, :] = x1h + dot(hd, wd_ref[...])


def _tail(x2, osb, ogla, gs, gg, wsb, wgla, wout, g2, wg, wu, wd):
    t = x2.shape[0]
    tm = ROW_TILE
    row = pl.BlockSpec((tm, D_MODEL), lambda i: (i, 0))
    consts = (wsb, wgla, wout, g2, wg, wu, wd)
    return pl.pallas_call(
        _tail_kernel,
        grid=(t // tm,),
        in_specs=[row] * 5 + [_const_spec(c.shape) for c in consts],
        out_specs=row,
        out_shape=jax.ShapeDtypeStruct((t, D_MODEL), F32),
        compiler_params=pltpu.CompilerParams(dimension_semantics=("arbitrary",),
                                             vmem_limit_bytes=VMEM_LIMIT_BYTES),
        name="merge_out_ffn",
    )(x2, osb, ogla, gs, gg, *consts)


def _layer(x, norm1_g, w_in, sb_q_norm_g, sb_k_norm_g, gla_gate_w2, gla_gate_b, gla_out_norm_g,
           w_branch_sb, w_branch_gla, w_out, norm2_g, w_ffn_gate, w_ffn_up, w_ffn_down):
    b, s, d = x.shape
    t = b * s
    x2 = x.reshape(t, d)

    lr0 = 3 * SB_WIDTH + 2 * GLA_KEY_WIDTH + 2 * GLA_VALUE_WIDTH
    lr1 = lr0 + GLA_GATE_RANK
    wa = w_in[:, :lr0].astype(BF16)
    wb = w_in[:, lr1:].astype(BF16)
    wlr = jnp.pad(w_in[:, lr0:lr1], ((0, 0), (0, LANES - GLA_GATE_RANK))).astype(BF16)
    w2p = jnp.pad(gla_gate_w2, ((0, LANES - GLA_GATE_RANK), (0, 0))).astype(BF16)

    heads_per_tile = MXU_DIM // SB_HEAD_DIM
    qg = jnp.tile(sb_q_norm_g * (SB_HEAD_DIM ** -0.5 * LOG2E), heads_per_tile).reshape(1, MXU_DIM)
    kg = jnp.tile(sb_k_norm_g, heads_per_tile).reshape(1, MXU_DIM)
    hid = np.arange(MXU_DIM) // SB_HEAD_DIM
    avg = jnp.asarray((hid[:, None] == hid[None, :]).astype(np.float32) / SB_HEAD_DIM, BF16)

    q, k, v, gq, gk, gv, sr, la, gs, gg = _in_projection(
        x2, norm1_g.reshape(1, d), wa, wb, wlr, w2p, gla_gate_b.reshape(1, -1), qg, kg, avg)

    idx = np.arange(SB_TK)
    mt = jnp.asarray(-(idx[None, :] > idx[:, None]).astype(np.float32), BF16)
    o_sb = _sb_attention(q.reshape(b, s, -1), k.reshape(b, s, -1), v.reshape(b, s, -1), mt)

    cm = jnp.asarray(_gla_level_matrix(), BF16)
    o_gla = _gla(gq.reshape(b, s, -1), gk.reshape(b, s, -1), gv.reshape(b, s, -1),
                 la.reshape(b, s, -1), sr.reshape(b, s, -1), cm, gla_out_norm_g.reshape(1, -1))

    out = _tail(x2, o_sb.reshape(t, -1), o_gla.reshape(t, -1), gs, gg,
                w_branch_sb.astype(BF16), w_branch_gla.astype(BF16), w_out.astype(BF16),
                norm2_g.reshape(1, d), w_ffn_gate.astype(BF16), w_ffn_up.astype(BF16),
                w_ffn_down.astype(BF16))
    return out.reshape(b, s, d)


def kernel(x, norm1_g, w_in, sb_q_norm_g, sb_k_norm_g, gla_gate_w2, gla_gate_b, gla_out_norm_g,
           w_branch_sb, w_branch_gla, w_out, norm2_g, w_ffn_gate, w_ffn_up, w_ffn_down):
    for l in range(norm1_g.shape[0]):
        x = _layer(x, norm1_g[l], w_in[l], sb_q_norm_g[l], sb_k_norm_g[l], gla_gate_w2[l],
                   gla_gate_b[l], gla_out_norm_g[l], w_branch_sb[l], w_branch_gla[l], w_out[l],
                   norm2_g[l], w_ffn_gate[l], w_ffn_up[l], w_ffn_down[l])
    return x
```

```python
import functools

import jax
import jax.numpy as jnp
import numpy as np
from jax import lax
from jax.experimental import pallas as pl
from jax.experimental.pallas import tpu as pltpu

F32 = jnp.float32
BF16 = jnp.bfloat16

D_MODEL = 1024
SB_HEADS = 16
SB_HEAD_DIM = 64
SB_WIDTH = SB_HEADS * SB_HEAD_DIM
GLA_HEADS = 4
GLA_KEY_WIDTH = D_MODEL // 2
GLA_VALUE_WIDTH = D_MODEL
GLA_DK = GLA_KEY_WIDTH // GLA_HEADS
GLA_DV = GLA_VALUE_WIDTH // GLA_HEADS
GLA_GATE_RANK = 16
GLA_GATE_TAU = 16.0
D_FF = -(-8 * D_MODEL // (3 * 256)) * 256
RMS_EPS = 1e-6
LOG2E = 1.4426950408889634

LANES = 128
MXU_DIM = 256
VMEM_LIMIT_BYTES = 56 * 1024 * 1024

ROW_TILE = 512
INPROJ_ROW_TILE = 512
SB_TQ = 4096
SB_TK = 256
SB_QT = 128
SB_UNDERFLOW_LOG2 = -126.0
SB_NO_BLOCK_BIAS = -1e30
GLA_CHUNK = 64
GLA_STEP = 512
GLA_LEVELS = (32, 16, 8, 4, 2, 1)


def _const_spec(shape):
    nd = len(shape)
    return pl.BlockSpec(shape, lambda *_: (0,) * nd, pipeline_mode=pl.Buffered(1))


def _log_sigmoid(x):
    return jnp.minimum(x, 0.0) - jnp.log(1.0 + jnp.exp(-jnp.abs(x)))


def _inproj_kernel(x_ref, g1_ref, wa_ref, wb_ref, wlr_ref, w2_ref, gb_ref, qg_ref, kg_ref, avg_ref,
                   q_ref, k_ref, v_ref, gq_ref, gk_ref, gv_ref, r_ref, la_ref, gs_ref, gg_ref):
    x = x_ref[...]
    ms = jnp.mean(x * x, axis=-1, keepdims=True)
    h = (x * lax.rsqrt(ms + RMS_EPS) * g1_ref[...]).astype(BF16)
    wa_cols = wa_ref.shape[1]

    def proj(c0, cw):
        w = wa_ref[:, c0:c0 + cw] if c0 < wa_cols else wb_ref[:, c0 - wa_cols:c0 - wa_cols + cw]
        return jnp.dot(h, w, preferred_element_type=F32)

    def head_norm(y, gain):
        m = jnp.dot((y * y).astype(BF16), avg_ref[...], preferred_element_type=F32)
        return y * lax.rsqrt(m + RMS_EPS) * gain

    segments = (
        (q_ref, SB_WIDTH, MXU_DIM, lambda y: head_norm(y, qg_ref[...])),
        (k_ref, SB_WIDTH, MXU_DIM, lambda y: head_norm(y, kg_ref[...])),
        (v_ref, SB_WIDTH, 512, lambda y: y),
        (gq_ref, GLA_KEY_WIDTH, 512, lambda y: y * (GLA_DK ** -0.5)),
        (gk_ref, GLA_KEY_WIDTH, 512, lambda y: y),
        (gv_ref, GLA_VALUE_WIDTH, 512, lambda y: y),
        (r_ref, GLA_VALUE_WIDTH, 512, lambda y: y * jax.nn.sigmoid(y)),
        (gs_ref, D_MODEL, 512, jax.nn.sigmoid),
        (gg_ref, D_MODEL, 512, jax.nn.sigmoid),
    )
    jobs, off = [], 0
    for ref, width, cw, post in segments:
        jobs += [(ref, lo, off + lo, cw, post) for lo in range(0, width, cw)]
        off += width

    lr = jnp.dot(h, wlr_ref[...], preferred_element_type=F32)
    y_next = proj(jobs[0][2], jobs[0][3])
    for n, (ref, lo, _, cw, post) in enumerate(jobs):
        y = y_next
        if n + 1 < len(jobs):
            y_next = proj(jobs[n + 1][2], jobs[n + 1][3])
        if n == 0:
            logit = jnp.dot(lr.astype(BF16), w2_ref[...], preferred_element_type=F32) + gb_ref[...]
        ref[:, lo:lo + cw] = post(y).astype(BF16)
    la_ref[...] = _log_sigmoid(logit) * (1.0 / GLA_GATE_TAU)


def _in_projection(x2, g1, wa, wb, wlr, w2p, gb, qg, kg, avg):
    t = x2.shape[0]
    tm = INPROJ_ROW_TILE
    row = lambda width: pl.BlockSpec((tm, width), lambda i: (i, 0))
    bf = lambda width: jax.ShapeDtypeStruct((t, width), BF16)
    out_shape = (bf(SB_WIDTH), bf(SB_WIDTH), bf(SB_WIDTH),
                 bf(GLA_KEY_WIDTH), bf(GLA_KEY_WIDTH), bf(GLA_VALUE_WIDTH), bf(GLA_VALUE_WIDTH),
                 jax.ShapeDtypeStruct((t, GLA_KEY_WIDTH), F32), bf(D_MODEL), bf(D_MODEL))
    out_specs = (row(SB_WIDTH), row(SB_WIDTH), row(SB_WIDTH),
                 row(GLA_KEY_WIDTH), row(GLA_KEY_WIDTH), row(GLA_VALUE_WIDTH), row(GLA_VALUE_WIDTH),
                 row(GLA_KEY_WIDTH), row(D_MODEL), row(D_MODEL))
    return pl.pallas_call(
        _inproj_kernel,
        grid=(t // tm,),
        in_specs=[row(D_MODEL)] + [_const_spec(c.shape) for c in (g1, wa, wb, wlr, w2p, gb, qg, kg, avg)],
        out_specs=out_specs,
        out_shape=out_shape,
        compiler_params=pltpu.CompilerParams(dimension_semantics=("arbitrary",),
                                             vmem_limit_bytes=VMEM_LIMIT_BYTES),
        name="in_projection",
    )(x2, g1, wa, wb, wlr, w2p, gb, qg, kg, avg)


def _sb_kernel(q_ref, k_ref, v_ref, mt_ref, o_ref, acc_ref, carry_ref):
    qi = pl.program_id(2)
    tq, tk, qt = q_ref.shape[1], SB_TK, SB_QT
    sub = tq // tk
    nt = (((1,), (1,)), ((), ()))
    tn = (((0,), (0,)), ((), ()))
    mt = mt_ref[...]
    one = jnp.ones((), BF16)
    zero16 = jnp.zeros((), BF16)
    first = qi * sub

    def head_split(q):
        lane = lax.broadcasted_iota(jnp.int32, q.shape, 1)
        zero = jnp.zeros_like(q)
        return (jnp.where(lane < SB_HEAD_DIM, q, zero), jnp.where(lane >= SB_HEAD_DIM, q, zero))

    def key_block(ref, j):
        return ref[0, pl.ds(pl.multiple_of(j * tk, tk), tk), :]

    def softplus2(zt):
        l2 = jnp.log2((one + jnp.exp2(-jnp.abs(zt))).astype(F32)).astype(BF16)
        return jnp.maximum(zt, zero16) + l2

    def causal(rows, width):
        return (lax.broadcasted_iota(jnp.int32, (rows, width), 0)
                < lax.broadcasted_iota(jnp.int32, (rows, width), 1))

    top_mask = causal(qt, tk)
    br_mask = causal(qt, qt)
    dead = jnp.zeros((qt, qt), BF16)

    def with_dead_quadrant(top, br):
        return jnp.concatenate([top, jnp.concatenate([dead, br], axis=1)], axis=0)

    def diag_logits(g):
        kb = key_block(k_ref, first + g)
        qh = head_split(q_ref[0, g * tk:(g + 1) * tk, :])
        return [lax.dot_general(kb, qh[h], nt, preferred_element_type=F32).astype(BF16)
                for h in range(2)]

    def diag_log_not(zts):
        out = []
        for zt in zts:
            p_top = jnp.where(top_mask, softplus2(zt[:qt, :]), zero16)
            p_br = jnp.where(br_mask, softplus2(zt[qt:, qt:]), zero16)
            cs = jnp.dot(mt, with_dead_quadrant(p_top, p_br), preferred_element_type=F32)
            out.append((p_top, p_br, cs))
        return out

    def diag_weights(g, zts, pcs):
        vb = key_block(v_ref, first + g)
        out = []
        for h in range(2):
            p_top, p_br, cs = pcs[h]
            zt = zts[h]
            w_top = jnp.exp2((zt[:qt, :] - p_top) + cs[:qt, :].astype(BF16))
            w_br = jnp.exp2((zt[qt:, qt:] - p_br) + cs[qt:, qt:].astype(BF16))
            w = with_dead_quadrant(jnp.where(top_mask, w_top, zero16), jnp.where(br_mask, w_br, zero16))
            pv = lax.dot_general(vb, w, tn, preferred_element_type=F32)
            out.append((pv, cs[0:1, :] - p_top[0:1, :].astype(F32)))
        return out

    def tile_logits(j, qh):
        kb = key_block(k_ref, j)
        return lax.dot_general(kb, jnp.concatenate(qh, axis=0), nt,
                               preferred_element_type=F32).astype(BF16)

    def tile_log_not(zt):
        p = softplus2(zt)
        return p, jnp.dot(mt, p, preferred_element_type=F32)

    def tile_weights(j, zt, pc, rows):
        vb = key_block(v_ref, j)
        p, cs = pc
        w = jnp.exp2(((zt - p) + cs.astype(BF16)) + jnp.concatenate(rows, axis=1).astype(BF16))
        pv = lax.dot_general(vb, w, tn, preferred_element_type=F32)
        tot = cs[0:1, :] - p[0:1, :].astype(F32)
        return [(pv[:, h * qt:(h + 1) * qt], tot[:, h * qt:(h + 1) * qt]) for h in range(2)]

    no_prev_bias = jnp.where(first > 0, 0.0, SB_NO_BLOCK_BIAS).astype(F32)
    prev_block = [jnp.maximum(first + g - 1, 0) for g in range(sub)]
    prev_q = [None] * sub
    n_units = 2 * sub
    zs, pcs, diag_out = {}, {}, {}

    def issue_logits(n):
        if n < sub:
            zs[n] = diag_logits(n)
        else:
            g = n - sub
            prev_q[g] = head_split(q_ref[0, g * tk:g * tk + qt, :])
            zs[n] = tile_logits(prev_block[g], prev_q[g])

    def issue_weights(n):
        if n < sub:
            diag_out[n] = diag_weights(n, zs.pop(n), pcs.pop(n))
            return
        g = n - sub
        rows = [diag_out[g][h][1][:, :qt] for h in range(2)]
        if g == 0:
            rows = [r + no_prev_bias for r in rows]
        res = tile_weights(prev_block[g], zs.pop(n), pcs.pop(n), rows)
        for h in range(2):
            pv_d, tot_d = diag_out[g][h]
            pv_p, tot_p = res[h]
            acc_ref[h, 2 * g] = pv_d[:, :qt] + pv_p
            acc_ref[h, 2 * g + 1] = pv_d[:, qt:]
            carry_ref[h, 2 * g:2 * g + 1, :] = tot_d[:, :qt] + tot_p
            carry_ref[h, 2 * g + 1:2 * g + 2, :] = tot_d[:, qt:]

    issue_logits(0)
    for n in range(n_units):
        if n + 1 < n_units:
            issue_logits(n + 1)
        pcs[n] = diag_log_not(zs[n]) if n < sub else tile_log_not(zs[n])
        if n >= 1:
            issue_weights(n - 1)
    issue_weights(n_units - 1)

    n_tiles = tq // qt
    tile_ids = lax.broadcasted_iota(jnp.int32, (n_tiles, 1), 0)
    next_block = first + (tile_ids >> 1) - 2 + (tile_ids & 1)

    def next_live_tile(after):
        top = jnp.max(jnp.maximum(carry_ref[0], carry_ref[1]), axis=1, keepdims=True)
        todo = (top >= SB_UNDERFLOW_LOG2) & (next_block >= 0) & (tile_ids > after)
        return jnp.min(jnp.where(todo, tile_ids, n_tiles))

    def walk_tile(c):
        qh = head_split(q_ref[0, pl.ds(pl.multiple_of(c * qt, qt), qt), :])

        def live(state):
            j, top = state
            return jnp.logical_and(j >= 0, top >= SB_UNDERFLOW_LOG2)

        def step(state):
            j, _ = state
            zts = tile_logits(j, qh)
            rows = [carry_ref[h, pl.ds(c, 1), :] for h in range(2)]
            res = tile_weights(j, zts, tile_log_not(zts), rows)
            new_rows = [rows[h] + res[h][1] for h in range(2)]
            for h in range(2):
                acc_ref[h, c] += res[h][0]
                carry_ref[h, pl.ds(c, 1), :] = new_rows[h]
            return j - 1, jnp.max(jnp.maximum(new_rows[0], new_rows[1]))

        lax.while_loop(live, step, (first + (c >> 1) - 2 + (c & 1), jnp.float32(0.0)))
        return next_live_tile(c)

    lax.while_loop(lambda c: c < n_tiles, walk_tile, next_live_tile(-1))

    row = lax.broadcasted_iota(jnp.int32, (LANES, qt), 0)
    for c in range(tq // qt):
        ot = jnp.where(row < SB_HEAD_DIM, acc_ref[0, c], acc_ref[1, c])
        o_ref[0, c * qt:(c + 1) * qt, :] = ot.T.astype(BF16)


def _sb_attention(q, k, v, mt):
    b, s, _ = q.shape
    tq = min(SB_TQ, s)
    qspec = pl.BlockSpec((1, tq, LANES), lambda bi, hp, i: (bi, i, hp))
    kvspec = pl.BlockSpec((1, s, LANES), lambda bi, hp, i: (bi, 0, hp))
    return pl.pallas_call(
        _sb_kernel,
        grid=(b, SB_WIDTH // LANES, s // tq),
        in_specs=[qspec, kvspec, kvspec, _const_spec(mt.shape)],
        out_specs=qspec,
        out_shape=jax.ShapeDtypeStruct((b, s, SB_WIDTH), BF16),
        scratch_shapes=[pltpu.VMEM((2, tq // SB_QT, LANES, SB_QT), F32),
                        pltpu.VMEM((2, tq // SB_QT, SB_QT), F32)],
        compiler_params=pltpu.CompilerParams(
            dimension_semantics=("arbitrary", "arbitrary", "arbitrary"),
            vmem_limit_bytes=VMEM_LIMIT_BYTES),
        name="stick_breaking_attention",
    )(q, k, v, mt)


def _gla_pivot_rows(n):
    rows = np.arange(GLA_CHUNK)
    base = (rows // n) * n
    second = (rows // n) % 2 == 1
    return second, np.where(second, base - 1, base + n - 1)


def _gla_level_matrix():
    c = GLA_CHUNK
    r = np.arange(c)[:, None]
    m = np.arange(c)[None, :]
    mats = [m <= r, m > r]
    for n in GLA_LEVELS:
        _, idx = _gla_pivot_rows(n)
        lo = np.minimum(r[:, 0], idx)[:, None]
        hi = np.maximum(r[:, 0], idx)[:, None]
        mats.append((m > lo) & (m <= hi))
    one = np.concatenate(mats, axis=0).astype(np.float32)
    return np.concatenate([one, one], axis=1)


def _gla_kernel(q_ref, k_ref, v_ref, la_ref, r_ref, cm_ref, ng_ref, o_ref, state_ref):
    c = GLA_CHUNK
    nch = GLA_STEP // c

    @pl.when(pl.program_id(2) == 0)
    def _():
        state_ref[...] = jnp.zeros_like(state_ref)

    cm = cm_ref[...]
    ri = lax.broadcasted_iota(jnp.int32, (c, c), 0)
    ci = lax.broadcasted_iota(jnp.int32, (c, c), 1)
    nt = (((1,), (1,)), ((), ()))
    tn = (((0,), (0,)), ((), ()))
    sls = [slice(ch * c, (ch + 1) * c) for ch in range(nch)]

    level_masks = []
    for n in GLA_LEVELS:
        sh = n.bit_length() - 1
        level_masks.append(((ri >> sh) == (ci >> sh) + 1) & (((ri >> sh) & 1) == 1))

    def hi_lo(sl):
        g = la_ref[0, sl, :]
        ghi = g.astype(BF16)
        glo = (g - ghi.astype(F32)).astype(BF16)
        return jnp.concatenate([ghi, glo], axis=0)

    args = []
    for ch in range(0, nch, 2):
        pair = jnp.dot(cm, jnp.concatenate([hi_lo(sls[ch]), hi_lo(sls[ch + 1])], axis=1),
                       preferred_element_type=F32)
        args += [pair[:, :GLA_DK], pair[:, GLA_DK:]]

    o_intra, q_in, decays, upds = [], [], [], []
    for sl, arg in zip(sls, args):
        ex = jnp.exp(arg.astype(BF16))
        qb = q_ref[0, sl, :]
        kb = k_ref[0, sl, :]
        vb = v_ref[0, sl, :]
        sc = jnp.where(ri == ci, lax.dot_general(qb, kb, nt, preferred_element_type=F32), 0.0)
        for li in range(len(GLA_LEVELS)):
            ex_l = ex[(li + 2) * c:(li + 3) * c]
            s_l = lax.dot_general(qb * ex_l, kb * ex_l, nt, preferred_element_type=F32)
            sc = jnp.where(level_masks[li], s_l, sc)
        o_intra.append(jnp.dot(sc.astype(BF16), vb, preferred_element_type=F32))
        q_in.append(qb * ex[0:c])
        upds.append(lax.dot_general(kb * ex[c:2 * c], vb, tn, preferred_element_type=F32))
        last = arg[c - 1:c, :]
        decay = jnp.broadcast_to(jnp.exp(last), (GLA_DK, GLA_DK)).T
        decays.append(jnp.concatenate([decay, decay], axis=1))

    state = state_ref[...]
    for ch, sl in enumerate(sls):
        o = o_intra[ch] + jnp.dot(q_in[ch], state.astype(BF16), preferred_element_type=F32)
        state = state * decays[ch] + upds[ch]
        ms = jnp.mean(o * o, axis=-1, keepdims=True)
        y = o * lax.rsqrt(ms + RMS_EPS) * ng_ref[...]
        o_ref[0, sl, :] = (y * r_ref[0, sl, :].astype(F32)).astype(BF16)
    state_ref[...] = state


def _gla(gq, gk, gv, la, sr, cm, ng):
    b, s, _ = gq.shape
    ts = GLA_STEP
    kspec = pl.BlockSpec((1, ts, GLA_DK), lambda bi, h, t: (bi, t, h))
    vspec = pl.BlockSpec((1, ts, GLA_DV), lambda bi, h, t: (bi, t, h))
    return pl.pallas_call(
        _gla_kernel,
        grid=(b, GLA_HEADS, s // ts),
        in_specs=[kspec, kspec, vspec, kspec, vspec, _const_spec(cm.shape), _const_spec(ng.shape)],
        out_specs=vspec,
        out_shape=jax.ShapeDtypeStruct((b, s, GLA_VALUE_WIDTH), BF16),
        scratch_shapes=[pltpu.VMEM((GLA_DK, GLA_DV), F32)],
        compiler_params=pltpu.CompilerParams(
            dimension_semantics=("arbitrary", "arbitrary", "arbitrary"),
            vmem_limit_bytes=VMEM_LIMIT_BYTES),
        name="gated_linear_attention",
    )(gq, gk, gv, la, sr, cm, ng)


def _tail_kernel(x_ref, osb_ref, ogla_ref, gs_ref, gg_ref, wsb_ref, wgla_ref, wout_ref,
                 g2_ref, wg_ref, wu_ref, wd_ref, o_ref):
    tm = x_ref.shape[0]
    halves = [slice(0, tm // 2), slice(tm // 2, tm)]
    dot = functools.partial(jnp.dot, preferred_element_type=F32)
    ab = [(dot(osb_ref[sl, :], wsb_ref[...]), dot(ogla_ref[sl, :], wgla_ref[...])) for sl in halves]
    merged = [(gs_ref[sl, :].astype(F32) * a + gg_ref[sl, :].astype(F32) * b).astype(BF16)
              for sl, (a, b) in zip(halves, ab)]
    x1 = [x_ref[sl, :] + dot(m, wout_ref[...]) for sl, m in zip(halves, merged)]
    h2 = []
    for x1h in x1:
        ms = jnp.mean(x1h * x1h, axis=-1, keepdims=True)
        h2.append((x1h * lax.rsqrt(ms + RMS_EPS) * g2_ref[...]).astype(BF16))
    gu = [(dot(h, wg_ref[...]), dot(h, wu_ref[...])) for h in h2]
    hid = [(gate * jax.nn.sigmoid(gate) * up).astype(BF16) for gate, up in gu]
    for sl, x1h, hd in zip(halves, x1, hid):
        o_ref[sl, :] = x1h + dot(hd, wd_ref[...])


def _tail(x2, osb, ogla, gs, gg, wsb, wgla, wout, g2, wg, wu, wd):
    t = x2.shape[0]
    tm = ROW_TILE
    row = pl.BlockSpec((tm, D_MODEL), lambda i: (i, 0))
    consts = (wsb, wgla, wout, g2, wg, wu, wd)
    return pl.pallas_call(
        _tail_kernel,
        grid=(t // tm,),
        in_specs=[row] * 5 + [_const_spec(c.shape) for c in consts],
        out_specs=row,
        out_shape=jax.ShapeDtypeStruct((t, D_MODEL), F32),
        compiler_params=pltpu.CompilerParams(dimension_semantics=("arbitrary",),
                                             vmem_limit_bytes=VMEM_LIMIT_BYTES),
        name="merge_out_ffn",
    )(x2, osb, ogla, gs, gg, *consts)


def _layer(x, norm1_g, w_in, sb_q_norm_g, sb_k_norm_g, gla_gate_w2, gla_gate_b, gla_out_norm_g,
           w_branch_sb, w_branch_gla, w_out, norm2_g, w_ffn_gate, w_ffn_up, w_ffn_down):
    b, s, d = x.shape
    t = b * s
    x2 = x.reshape(t, d)

    lr0 = 3 * SB_WIDTH + 2 * GLA_KEY_WIDTH + 2 * GLA_VALUE_WIDTH
    lr1 = lr0 + GLA_GATE_RANK
    wa = w_in[:, :lr0].astype(BF16)
    wb = w_in[:, lr1:].astype(BF16)
    wlr = jnp.pad(w_in[:, lr0:lr1], ((0, 0), (0, LANES - GLA_GATE_RANK))).astype(BF16)
    w2p = jnp.pad(gla_gate_w2, ((0, LANES - GLA_GATE_RANK), (0, 0))).astype(BF16)

    heads_per_tile = MXU_DIM // SB_HEAD_DIM
    qg = jnp.tile(sb_q_norm_g * (SB_HEAD_DIM ** -0.5 * LOG2E), heads_per_tile).reshape(1, MXU_DIM)
    kg = jnp.tile(sb_k_norm_g, heads_per_tile).reshape(1, MXU_DIM)
    hid = np.arange(MXU_DIM) // SB_HEAD_DIM
    avg = jnp.asarray((hid[:, None] == hid[None, :]).astype(np.float32) / SB_HEAD_DIM, BF16)

    q, k, v, gq, gk, gv, sr, la, gs, gg = _in_projection(
        x2, norm1_g.reshape(1, d), wa, wb, wlr, w2p, gla_gate_b.reshape(1, -1), qg, kg, avg)

    idx = np.arange(SB_TK)
    mt = jnp.asarray(-(idx[None, :] > idx[:, None]).astype(np.float32), BF16)
    o_sb = _sb_attention(q.reshape(b, s, -1), k.reshape(b, s, -1), v.reshape(b, s, -1), mt)

    cm = jnp.asarray(_gla_level_matrix(), BF16)
    o_gla = _gla(gq.reshape(b, s, -1), gk.reshape(b, s, -1), gv.reshape(b, s, -1),
                 la.reshape(b, s, -1), sr.reshape(b, s, -1), cm, gla_out_norm_g.reshape(1, -1))

    out = _tail(x2, o_sb.reshape(t, -1), o_gla.reshape(t, -1), gs, gg,
                w_branch_sb.astype(BF16), w_branch_gla.astype(BF16), w_out.astype(BF16),
                norm2_g.reshape(1, d), w_ffn_gate.astype(BF16), w_ffn_up.astype(BF16),
                w_ffn_down.astype(BF16))
    return out.reshape(b, s, d)


def kernel(x, norm1_g, w_in, sb_q_norm_g, sb_k_norm_g, gla_gate_w2, gla_gate_b, gla_out_norm_g,
           w_branch_sb, w_branch_gla, w_out, norm2_g, w_ffn_gate, w_ffn_up, w_ffn_down):
    for l in range(norm1_g.shape[0]):
        x = _layer(x, norm1_g[l], w_in[l], sb_q_norm_g[l], sb_k_norm_g[l], gla_gate_w2[l],
                   gla_gate_b[l], gla_out_norm_g[l], w_branch_sb[l], w_branch_gla[l], w_out[l],
                   norm2_g[l], w_ffn_gate[l], w_ffn_up[l], w_ffn_down[l])
    return x
```

```python
import functools

import jax
import jax.numpy as jnp
import numpy as np
from jax import lax
from jax.experimental import pallas as pl
from jax.experimental.pallas import tpu as pltpu

F32 = jnp.float32
BF16 = jnp.bfloat16

D_MODEL = 1024
SB_HEADS = 16
SB_HEAD_DIM = 64
SB_WIDTH = SB_HEADS * SB_HEAD_DIM
GLA_HEADS = 4
GLA_KEY_WIDTH = D_MODEL // 2
GLA_VALUE_WIDTH = D_MODEL
GLA_DK = GLA_KEY_WIDTH // GLA_HEADS
GLA_DV = GLA_VALUE_WIDTH // GLA_HEADS
GLA_GATE_RANK = 16
GLA_GATE_TAU = 16.0
D_FF = -(-8 * D_MODEL // (3 * 256)) * 256
RMS_EPS = 1e-6
LOG2E = 1.4426950408889634

LANES = 128
MXU_DIM = 256
VMEM_LIMIT_BYTES = 56 * 1024 * 1024

ROW_TILE = 512
INPROJ_ROW_TILE = 512
INPROJ_CHUNK = 512
SB_TQ = 4096
SB_TK = 256
SB_QT = 128
SB_UNDERFLOW_LOG2 = -126.0
SB_NO_BLOCK_BIAS = -1e30
GLA_CHUNK = 64
GLA_STEP = 512
GLA_HEADS_PER_STEP = 4
GLA_LEVELS = (32, 16, 8, 4, 2, 1)


def _const_spec(shape):
    nd = len(shape)
    return pl.BlockSpec(shape, lambda *_: (0,) * nd, pipeline_mode=pl.Buffered(1))


def _log_sigmoid(x):
    return jnp.minimum(x, 0.0) - jnp.log(1.0 + jnp.exp(-jnp.abs(x)))


def _inproj_kernel(x_ref, g1_ref, wa_ref, wb_ref, wlr_ref, w2_ref, gb_ref, qg_ref, kg_ref, avg_ref,
                   q_ref, k_ref, v_ref, gq_ref, gk_ref, gv_ref, r_ref, la_ref, gs_ref, gg_ref):
    x = x_ref[...]
    ms = jnp.mean(x * x, axis=-1, keepdims=True)
    h = (x * lax.rsqrt(ms + RMS_EPS) * g1_ref[...]).astype(BF16)
    wa_cols = wa_ref.shape[1]

    def proj(c0, cw):
        w = wa_ref[:, c0:c0 + cw] if c0 < wa_cols else wb_ref[:, c0 - wa_cols:c0 - wa_cols + cw]
        return jnp.dot(h, w, preferred_element_type=F32)

    def head_norm(y, gain):
        m = jnp.dot((y * y).astype(BF16), avg_ref[...], preferred_element_type=F32)
        return y * lax.rsqrt(m + RMS_EPS) * gain

    segments = (
        (q_ref, SB_WIDTH, MXU_DIM, lambda y: head_norm(y, qg_ref[...])),
        (k_ref, SB_WIDTH, MXU_DIM, lambda y: head_norm(y, kg_ref[...])),
        (v_ref, SB_WIDTH, INPROJ_CHUNK, lambda y: y),
        (gq_ref, GLA_KEY_WIDTH, INPROJ_CHUNK, lambda y: y * (GLA_DK ** -0.5)),
        (gk_ref, GLA_KEY_WIDTH, INPROJ_CHUNK, lambda y: y),
        (gv_ref, GLA_VALUE_WIDTH, INPROJ_CHUNK, lambda y: y),
        (r_ref, GLA_VALUE_WIDTH, INPROJ_CHUNK, lambda y: y * jax.nn.sigmoid(y)),
        (gs_ref, D_MODEL, INPROJ_CHUNK, jax.nn.sigmoid),
        (gg_ref, D_MODEL, INPROJ_CHUNK, jax.nn.sigmoid),
    )
    normed, plain, off = [], [], 0
    for ref, width, cw, post in segments:
        cw = min(cw, width)
        (normed if (ref is q_ref or ref is k_ref) else plain).extend(
            (ref, lo, off + lo, cw, post) for lo in range(0, width, cw))
        off += width
    jobs = []
    while normed or plain:
        if plain:
            jobs.append(plain.pop(0))
        if normed:
            jobs.append(normed.pop(0))

    lr = jnp.dot(h, wlr_ref[...], preferred_element_type=F32)
    y_next = proj(jobs[0][2], jobs[0][3])
    for n, (ref, lo, _, cw, post) in enumerate(jobs):
        y = y_next
        if n + 1 < len(jobs):
            y_next = proj(jobs[n + 1][2], jobs[n + 1][3])
        if n == 0:
            logit = jnp.dot(lr.astype(BF16), w2_ref[...], preferred_element_type=F32) + gb_ref[...]
        ref[:, lo:lo + cw] = post(y).astype(BF16)
    la_ref[...] = _log_sigmoid(logit) * (1.0 / GLA_GATE_TAU)


def _in_projection(x2, g1, wa, wb, wlr, w2p, gb, qg, kg, avg):
    t = x2.shape[0]
    tm = INPROJ_ROW_TILE
    row = lambda width: pl.BlockSpec((tm, width), lambda i: (i, 0))
    bf = lambda width: jax.ShapeDtypeStruct((t, width), BF16)
    out_shape = (bf(SB_WIDTH), bf(SB_WIDTH), bf(SB_WIDTH),
                 bf(GLA_KEY_WIDTH), bf(GLA_KEY_WIDTH), bf(GLA_VALUE_WIDTH), bf(GLA_VALUE_WIDTH),
                 jax.ShapeDtypeStruct((t, GLA_KEY_WIDTH), F32), bf(D_MODEL), bf(D_MODEL))
    out_specs = (row(SB_WIDTH), row(SB_WIDTH), row(SB_WIDTH),
                 row(GLA_KEY_WIDTH), row(GLA_KEY_WIDTH), row(GLA_VALUE_WIDTH), row(GLA_VALUE_WIDTH),
                 row(GLA_KEY_WIDTH), row(D_MODEL), row(D_MODEL))
    return pl.pallas_call(
        _inproj_kernel,
        grid=(t // tm,),
        in_specs=[row(D_MODEL)] + [_const_spec(c.shape) for c in (g1, wa, wb, wlr, w2p, gb, qg, kg, avg)],
        out_specs=out_specs,
        out_shape=out_shape,
        compiler_params=pltpu.CompilerParams(dimension_semantics=("arbitrary",),
                                             vmem_limit_bytes=VMEM_LIMIT_BYTES),
        name="in_projection",
    )(x2, g1, wa, wb, wlr, w2p, gb, qg, kg, avg)


def _sb_kernel(q_ref, k_ref, v_ref, mt_ref, o_ref, acc_ref, carry_ref):
    qi = pl.program_id(2)
    tq, tk, qt = q_ref.shape[1], SB_TK, SB_QT
    sub = tq // tk
    nt = (((1,), (1,)), ((), ()))
    tn = (((0,), (0,)), ((), ()))
    mt = mt_ref[...]
    one = jnp.ones((), BF16)
    zero16 = jnp.zeros((), BF16)
    first = qi * sub

    def head_split(q):
        lane = lax.broadcasted_iota(jnp.int32, q.shape, 1)
        zero = jnp.zeros_like(q)
        return (jnp.where(lane < SB_HEAD_DIM, q, zero), jnp.where(lane >= SB_HEAD_DIM, q, zero))

    def key_block(ref, j):
        return ref[0, pl.ds(pl.multiple_of(j * tk, tk), tk), :]

    def softplus2(zt):
        l2 = jnp.log2((one + jnp.exp2(-jnp.abs(zt))).astype(F32)).astype(BF16)
        return jnp.maximum(zt, zero16) + l2

    def causal(rows, width):
        return (lax.broadcasted_iota(jnp.int32, (rows, width), 0)
                < lax.broadcasted_iota(jnp.int32, (rows, width), 1))

    top_mask = causal(qt, tk)
    br_mask = causal(qt, qt)
    dead = jnp.zeros((qt, qt), BF16)

    def with_dead_quadrant(top, br):
        return jnp.concatenate([top, jnp.concatenate([dead, br], axis=1)], axis=0)

    def diag_logits(g):
        kb = key_block(k_ref, first + g)
        qh = head_split(q_ref[0, g * tk:(g + 1) * tk, :])
        return [lax.dot_general(kb, qh[h], nt, preferred_element_type=F32).astype(BF16)
                for h in range(2)]

    def diag_log_not(zts):
        out = []
        for zt in zts:
            p_top = jnp.where(top_mask, softplus2(zt[:qt, :]), zero16)
            p_br = jnp.where(br_mask, softplus2(zt[qt:, qt:]), zero16)
            cs = jnp.dot(mt, with_dead_quadrant(p_top, p_br), preferred_element_type=F32)
            out.append((p_top, p_br, cs))
        return out

    def diag_weights(g, zts, pcs):
        vb = key_block(v_ref, first + g)
        out = []
        for h in range(2):
            p_top, p_br, cs = pcs[h]
            zt = zts[h]
            w_top = jnp.exp2((zt[:qt, :] - p_top) + cs[:qt, :].astype(BF16))
            w_br = jnp.exp2((zt[qt:, qt:] - p_br) + cs[qt:, qt:].astype(BF16))
            w = with_dead_quadrant(jnp.where(top_mask, w_top, zero16), jnp.where(br_mask, w_br, zero16))
            pv = lax.dot_general(vb, w, tn, preferred_element_type=F32)
            out.append((pv, cs[0:1, :] - p_top[0:1, :].astype(F32)))
        return out

    def tile_logits(j, qh):
        kb = key_block(k_ref, j)
        return lax.dot_general(kb, jnp.concatenate(qh, axis=0), nt,
                               preferred_element_type=F32).astype(BF16)

    def tile_log_not(zt):
        p = softplus2(zt)
        return p, jnp.dot(mt, p, preferred_element_type=F32)

    def tile_weights(j, zt, pc, rows):
        vb = key_block(v_ref, j)
        p, cs = pc
        w = jnp.exp2(((zt - p) + cs.astype(BF16)) + jnp.concatenate(rows, axis=1).astype(BF16))
        pv = lax.dot_general(vb, w, tn, preferred_element_type=F32)
        tot = cs[0:1, :] - p[0:1, :].astype(F32)
        return [(pv[:, h * qt:(h + 1) * qt], tot[:, h * qt:(h + 1) * qt]) for h in range(2)]

    no_prev_bias = jnp.where(first > 0, 0.0, SB_NO_BLOCK_BIAS).astype(F32)
    prev_block = [jnp.maximum(first + g - 1, 0) for g in range(sub)]
    prev_q = [None] * sub
    n_units = 2 * sub
    zs, pcs, diag_out = {}, {}, {}

    def issue_logits(n):
        if n < sub:
            zs[n] = diag_logits(n)
        else:
            g = n - sub
            prev_q[g] = head_split(q_ref[0, g * tk:g * tk + qt, :])
            zs[n] = tile_logits(prev_block[g], prev_q[g])

    def issue_weights(n):
        if n < sub:
            diag_out[n] = diag_weights(n, zs.pop(n), pcs.pop(n))
            return
        g = n - sub
        rows = [diag_out[g][h][1][:, :qt] for h in range(2)]
        if g == 0:
            rows = [r + no_prev_bias for r in rows]
        res = tile_weights(prev_block[g], zs.pop(n), pcs.pop(n), rows)
        for h in range(2):
            pv_d, tot_d = diag_out[g][h]
            pv_p, tot_p = res[h]
            acc_ref[h, 2 * g] = pv_d[:, :qt] + pv_p
            acc_ref[h, 2 * g + 1] = pv_d[:, qt:]
            carry_ref[h, 2 * g:2 * g + 1, :] = tot_d[:, :qt] + tot_p
            carry_ref[h, 2 * g + 1:2 * g + 2, :] = tot_d[:, qt:]

    issue_logits(0)
    for n in range(n_units):
        if n + 1 < n_units:
            issue_logits(n + 1)
        pcs[n] = diag_log_not(zs[n]) if n < sub else tile_log_not(zs[n])
        if n >= 1:
            issue_weights(n - 1)
    issue_weights(n_units - 1)

    n_tiles = tq // qt
    tile_ids = lax.broadcasted_iota(jnp.int32, (n_tiles, 1), 0)
    next_block = first + (tile_ids >> 1) - 2 + (tile_ids & 1)

    def next_live_tile(after):
        top = jnp.max(jnp.maximum(carry_ref[0], carry_ref[1]), axis=1, keepdims=True)
        todo = (top >= SB_UNDERFLOW_LOG2) & (next_block >= 0) & (tile_ids > after)
        return jnp.min(jnp.where(todo, tile_ids, n_tiles))

    def walk_tile(c):
        qh = head_split(q_ref[0, pl.ds(pl.multiple_of(c * qt, qt), qt), :])

        def live(state):
            j, top = state
            return jnp.logical_and(j >= 0, top >= SB_UNDERFLOW_LOG2)

        def step(state):
            j, _ = state
            zts = tile_logits(j, qh)
            rows = [carry_ref[h, pl.ds(c, 1), :] for h in range(2)]
            res = tile_weights(j, zts, tile_log_not(zts), rows)
            new_rows = [rows[h] + res[h][1] for h in range(2)]
            for h in range(2):
                acc_ref[h, c] += res[h][0]
                carry_ref[h, pl.ds(c, 1), :] = new_rows[h]
            return j - 1, jnp.max(jnp.maximum(new_rows[0], new_rows[1]))

        lax.while_loop(live, step, (first + (c >> 1) - 2 + (c & 1), jnp.float32(0.0)))
        return next_live_tile(c)

    lax.while_loop(lambda c: c < n_tiles, walk_tile, next_live_tile(-1))

    row = lax.broadcasted_iota(jnp.int32, (LANES, qt), 0)
    for c in range(tq // qt):
        ot = jnp.where(row < SB_HEAD_DIM, acc_ref[0, c], acc_ref[1, c])
        o_ref[0, c * qt:(c + 1) * qt, :] = ot.T.astype(BF16)


def _sb_attention(q, k, v, mt):
    b, s, _ = q.shape
    tq = min(SB_TQ, s)
    qspec = pl.BlockSpec((1, tq, LANES), lambda bi, hp, i: (bi, i, hp))
    kvspec = pl.BlockSpec((1, s, LANES), lambda bi, hp, i: (bi, 0, hp))
    return pl.pallas_call(
        _sb_kernel,
        grid=(b, SB_WIDTH // LANES, s // tq),
        in_specs=[qspec, kvspec, kvspec, _const_spec(mt.shape)],
        out_specs=qspec,
        out_shape=jax.ShapeDtypeStruct((b, s, SB_WIDTH), BF16),
        scratch_shapes=[pltpu.VMEM((2, tq // SB_QT, LANES, SB_QT), F32),
                        pltpu.VMEM((2, tq // SB_QT, SB_QT), F32)],
        compiler_params=pltpu.CompilerParams(
            dimension_semantics=("arbitrary", "arbitrary", "arbitrary"),
            vmem_limit_bytes=VMEM_LIMIT_BYTES),
        name="stick_breaking_attention",
    )(q, k, v, mt)


def _gla_pivot_rows(n):
    rows = np.arange(GLA_CHUNK)
    base = (rows // n) * n
    second = (rows // n) % 2 == 1
    return second, np.where(second, base - 1, base + n - 1)


def _gla_level_matrix():
    c = GLA_CHUNK
    r = np.arange(c)[:, None]
    m = np.arange(c)[None, :]
    mats = [m <= r, m > r]
    for n in GLA_LEVELS:
        _, idx = _gla_pivot_rows(n)
        lo = np.minimum(r[:, 0], idx)[:, None]
        hi = np.maximum(r[:, 0], idx)[:, None]
        mats.append((m > lo) & (m <= hi))
    one = np.concatenate(mats, axis=0).astype(np.float32)
    return np.concatenate([one, one], axis=1)


def _gla_kernel(q_ref, k_ref, v_ref, la_ref, r_ref, cm_ref, ng_ref, o_ref, state_ref):
    c = GLA_CHUNK
    nch = GLA_STEP // c
    heads = state_ref.shape[0]

    @pl.when(pl.program_id(2) == 0)
    def _():
        state_ref[...] = jnp.zeros_like(state_ref)

    cm = cm_ref[...]
    ri = lax.broadcasted_iota(jnp.int32, (c, c), 0)
    ci = lax.broadcasted_iota(jnp.int32, (c, c), 1)
    nt = (((1,), (1,)), ((), ()))
    tn = (((0,), (0,)), ((), ()))
    units = [(hh, slice(ch * c, (ch + 1) * c), slice(hh * GLA_DK, (hh + 1) * GLA_DK),
              slice(hh * GLA_DV, (hh + 1) * GLA_DV)) for hh in range(heads) for ch in range(nch)]

    level_masks = []
    for n in GLA_LEVELS:
        sh = n.bit_length() - 1
        level_masks.append(((ri >> sh) == (ci >> sh) + 1) & (((ri >> sh) & 1) == 1))

    def hi_lo(rows, kl):
        g = la_ref[0, rows, kl]
        ghi = g.astype(BF16)
        glo = (g - ghi.astype(F32)).astype(BF16)
        return jnp.concatenate([ghi, glo], axis=0)

    args = []
    for u in range(0, len(units), 2):
        (_, r0, kl0, _), (_, r1, kl1, _) = units[u], units[u + 1]
        pair = jnp.dot(cm, jnp.concatenate([hi_lo(r0, kl0), hi_lo(r1, kl1)], axis=1),
                       preferred_element_type=F32)
        args += [pair[:, :GLA_DK], pair[:, GLA_DK:]]

    o_intra, q_in, decays, upds = [], [], [], []
    for (_, rows, kl, vl), arg in zip(units, args):
        ex = jnp.exp(arg.astype(BF16))
        qb = q_ref[0, rows, kl]
        kb = k_ref[0, rows, kl]
        vb = v_ref[0, rows, vl]
        sc = jnp.where(ri == ci, lax.dot_general(qb, kb, nt, preferred_element_type=F32), 0.0)
        for li in range(len(GLA_LEVELS)):
            ex_l = ex[(li + 2) * c:(li + 3) * c]
            s_l = lax.dot_general(qb * ex_l, kb * ex_l, nt, preferred_element_type=F32)
            sc = jnp.where(level_masks[li], s_l, sc)
        o_intra.append(jnp.dot(sc.astype(BF16), vb, preferred_element_type=F32))
        q_in.append(qb * ex[0:c])
        upds.append(lax.dot_general(kb * ex[c:2 * c], vb, tn, preferred_element_type=F32))
        last = arg[c - 1:c, :]
        decay = jnp.broadcast_to(jnp.exp(last), (GLA_DK, GLA_DK)).T
        decays.append(jnp.concatenate([decay, decay], axis=1))

    states = [state_ref[hh] for hh in range(heads)]
    for u, (hh, rows, _, vl) in enumerate(units):
        state = states[hh]
        o = o_intra[u] + jnp.dot(q_in[u], state.astype(BF16), preferred_element_type=F32)
        states[hh] = state * decays[u] + upds[u]
        ms = jnp.mean(o * o, axis=-1, keepdims=True)
        y = o * lax.rsqrt(ms + RMS_EPS) * ng_ref[...]
        o_ref[0, rows, vl] = (y * r_ref[0, rows, vl].astype(F32)).astype(BF16)
    for hh in range(heads):
        state_ref[hh] = states[hh]


def _gla(gq, gk, gv, la, sr, cm, ng):
    b, s, _ = gq.shape
    ts = GLA_STEP
    hps = GLA_HEADS_PER_STEP
    kspec = pl.BlockSpec((1, ts, hps * GLA_DK), lambda bi, h, t: (bi, t, h))
    vspec = pl.BlockSpec((1, ts, hps * GLA_DV), lambda bi, h, t: (bi, t, h))
    return pl.pallas_call(
        _gla_kernel,
        grid=(b, GLA_HEADS // hps, s // ts),
        in_specs=[kspec, kspec, vspec, kspec, vspec, _const_spec(cm.shape), _const_spec(ng.shape)],
        out_specs=vspec,
        out_shape=jax.ShapeDtypeStruct((b, s, GLA_VALUE_WIDTH), BF16),
        scratch_shapes=[pltpu.VMEM((hps, GLA_DK, GLA_DV), F32)],
        compiler_params=pltpu.CompilerParams(
            dimension_semantics=("arbitrary", "arbitrary", "arbitrary"),
            vmem_limit_bytes=VMEM_LIMIT_BYTES),
        name="gated_linear_attention",
    )(gq, gk, gv, la, sr, cm, ng)


def _tail_kernel(x_ref, osb_ref, ogla_ref, gs_ref, gg_ref, wsb_ref, wgla_ref, wout_ref,
                 g2_ref, wg_ref, wu_ref, wd_ref, o_ref):
    tm = x_ref.shape[0]
    halves = [slice(0, tm // 2), slice(tm // 2, tm)]
    dot = functools.partial(jnp.dot, preferred_element_type=F32)
    ab = [(dot(osb_ref[sl, :], wsb_ref[...]), dot(ogla_ref[sl, :], wgla_ref[...])) for sl in halves]
    merged = [(gs_ref[sl, :].astype(F32) * a + gg_ref[sl, :].astype(F32) * b).astype(BF16)
              for sl, (a, b) in zip(halves, ab)]
    x1 = [x_ref[sl, :] + dot(m, wout_ref[...]) for sl, m in zip(halves, merged)]
    h2 = []
    for x1h in x1:
        ms = jnp.mean(x1h * x1h, axis=-1, keepdims=True)
        h2.append((x1h * lax.rsqrt(ms + RMS_EPS) * g2_ref[...]).astype(BF16))
    gu = [(dot(h, wg_ref[...]), dot(h, wu_ref[...])) for h in h2]
    hid = [(gate * jax.nn.sigmoid(gate) * up).astype(BF16) for gate, up in gu]
    for sl, x1h, hd in zip(halves, x1, hid):
        o_ref[sl, :] = x1h + dot(hd, wd_ref[...])


def _tail(x2, osb, ogla, gs, gg, wsb, wgla, wout, g2, wg, wu, wd):
    t = x2.shape[0]
    tm = ROW_TILE
    row = pl.BlockSpec((tm, D_MODEL), lambda i: (i, 0))
    consts = (wsb, wgla, wout, g2, wg, wu, wd)
    return pl.pallas_call(
        _tail_kernel,
        grid=(t // tm,),
        in_specs=[row] * 5 + [_const_spec(c.shape) for c in consts],
        out_specs=row,
        out_shape=jax.ShapeDtypeStruct((t, D_MODEL), F32),
        compiler_params=pltpu.CompilerParams(dimension_semantics=("arbitrary",),
                                             vmem_limit_bytes=VMEM_LIMIT_BYTES),
        name="merge_out_ffn",
    )(x2, osb, ogla, gs, gg, *consts)


def _layer(x, norm1_g, w_in, sb_q_norm_g, sb_k_norm_g, gla_gate_w2, gla_gate_b, gla_out_norm_g,
           w_branch_sb, w_branch_gla, w_out, norm2_g, w_ffn_gate, w_ffn_up, w_ffn_down):
    b, s, d = x.shape
    t = b * s
    x2 = x.reshape(t, d)

    lr0 = 3 * SB_WIDTH + 2 * GLA_KEY_WIDTH + 2 * GLA_VALUE_WIDTH
    lr1 = lr0 + GLA_GATE_RANK
    wa = w_in[:, :lr0].astype(BF16)
    wb = w_in[:, lr1:].astype(BF16)
    wlr = jnp.pad(w_in[:, lr0:lr1], ((0, 0), (0, LANES - GLA_GATE_RANK))).astype(BF16)
    w2p = jnp.pad(gla_gate_w2, ((0, LANES - GLA_GATE_RANK), (0, 0))).astype(BF16)

    heads_per_tile = MXU_DIM // SB_HEAD_DIM
    qg = jnp.tile(sb_q_norm_g * (SB_HEAD_DIM ** -0.5 * LOG2E), heads_per_tile).reshape(1, MXU_DIM)
    kg = jnp.tile(sb_k_norm_g, heads_per_tile).reshape(1, MXU_DIM)
    hid = np.arange(MXU_DIM) // SB_HEAD_DIM
    avg = jnp.asarray((hid[:, None] == hid[None, :]).astype(np.float32) / SB_HEAD_DIM, BF16)

    q, k, v, gq, gk, gv, sr, la, gs, gg = _in_projection(
        x2, norm1_g.reshape(1, d), wa, wb, wlr, w2p, gla_gate_b.reshape(1, -1), qg, kg, avg)

    idx = np.arange(SB_TK)
    mt = jnp.asarray(-(idx[None, :] > idx[:, None]).astype(np.float32), BF16)
    o_sb = _sb_attention(q.reshape(b, s, -1), k.reshape(b, s, -1), v.reshape(b, s, -1), mt)

    cm = jnp.asarray(_gla_level_matrix(), BF16)
    o_gla = _gla(gq.reshape(b, s, -1), gk.reshape(b, s, -1), gv.reshape(b, s, -1),
                 la.reshape(b, s, -1), sr.reshape(b, s, -1), cm, gla_out_norm_g.reshape(1, -1))

    out = _tail(x2, o_sb.reshape(t, -1), o_gla.reshape(t, -1), gs, gg,
                w_branch_sb.astype(BF16), w_branch_gla.astype(BF16), w_out.astype(BF16),
                norm2_g.reshape(1, d), w_ffn_gate.astype(BF16), w_ffn_up.astype(BF16),
                w_ffn_down.astype(BF16))
    return out.reshape(b, s, d)


def kernel(x, norm1_g, w_in, sb_q_norm_g, sb_k_norm_g, gla_gate_w2, gla_gate_b, gla_out_norm_g,
           w_branch_sb, w_branch_gla, w_out, norm2_g, w_ffn_gate, w_ffn_up, w_ffn_down):
    for l in range(norm1_g.shape[0]):
        x = _layer(x, norm1_g[l], w_in[l], sb_q_norm_g[l], sb_k_norm_g[l], gla_gate_w2[l],
                   gla_gate_b[l], gla_out_norm_g[l], w_branch_sb[l], w_branch_gla[l], w_out[l],
                   norm2_g[l], w_ffn_gate[l], w_ffn_up[l], w_ffn_down[l])
    return x
```

```python
import functools

import jax
import jax.numpy as jnp
import numpy as np
from jax import lax
from jax.experimental import pallas as pl
from jax.experimental.pallas import tpu as pltpu

F32 = jnp.float32
BF16 = jnp.bfloat16

D_MODEL = 1024
SB_HEADS = 16
SB_HEAD_DIM = 64
SB_WIDTH = SB_HEADS * SB_HEAD_DIM
GLA_HEADS = 4
GLA_KEY_WIDTH = D_MODEL // 2
GLA_VALUE_WIDTH = D_MODEL
GLA_DK = GLA_KEY_WIDTH // GLA_HEADS
GLA_DV = GLA_VALUE_WIDTH // GLA_HEADS
GLA_GATE_RANK = 16
GLA_GATE_TAU = 16.0
D_FF = -(-8 * D_MODEL // (3 * 256)) * 256
RMS_EPS = 1e-6
LOG2E = 1.4426950408889634

LANES = 128
MXU_DIM = 256
VMEM_LIMIT_BYTES = 56 * 1024 * 1024

ROW_TILE = 512
INPROJ_ROW_TILE = 512
INPROJ_CHUNK = 512
SB_TQ = 4096
SB_TK = 256
SB_QT = 128
SB_UNDERFLOW_LOG2 = -126.0
SB_NO_BLOCK_BIAS = -1e30
GLA_CHUNK = 64
GLA_LEVELS = (32, 16, 8, 4, 2, 1)


def _const_spec(shape):
    nd = len(shape)
    return pl.BlockSpec(shape, lambda *_: (0,) * nd, pipeline_mode=pl.Buffered(1))


def _log_sigmoid(x):
    return jnp.minimum(x, 0.0) - jnp.log(1.0 + jnp.exp(-jnp.abs(x)))


def _inproj_gla_kernel(x_ref, g1_ref, wa_ref, wb_ref, wlr_ref, w2_ref, gb_ref, qg_ref, kg_ref, avg_ref,
                       cm_ref, ng_ref,
                       q_ref, k_ref, v_ref, gs_ref, gg_ref, og_ref,
                       gq_s, gk_s, gv_s, sr_s, la_s, state_ref, *, steps_per_seq):
    tm = x_ref.shape[0]

    @pl.when(pl.program_id(0) % steps_per_seq == 0)
    def _():
        state_ref[...] = jnp.zeros_like(state_ref)

    x = x_ref[...]
    ms = jnp.mean(x * x, axis=-1, keepdims=True)
    h = (x * lax.rsqrt(ms + RMS_EPS) * g1_ref[...]).astype(BF16)
    wa_cols = wa_ref.shape[1]

    def proj(c0, cw):
        w = wa_ref[:, c0:c0 + cw] if c0 < wa_cols else wb_ref[:, c0 - wa_cols:c0 - wa_cols + cw]
        return jnp.dot(h, w, preferred_element_type=F32)

    def head_norm(y, gain):
        m = jnp.dot((y * y).astype(BF16), avg_ref[...], preferred_element_type=F32)
        return y * lax.rsqrt(m + RMS_EPS) * gain

    segments = (
        (q_ref, SB_WIDTH, MXU_DIM, lambda y: head_norm(y, qg_ref[...])),
        (k_ref, SB_WIDTH, MXU_DIM, lambda y: head_norm(y, kg_ref[...])),
        (v_ref, SB_WIDTH, INPROJ_CHUNK, lambda y: y),
        (gq_s, GLA_KEY_WIDTH, INPROJ_CHUNK, lambda y: y * (GLA_DK ** -0.5)),
        (gk_s, GLA_KEY_WIDTH, INPROJ_CHUNK, lambda y: y),
        (gv_s, GLA_VALUE_WIDTH, INPROJ_CHUNK, lambda y: y),
        (sr_s, GLA_VALUE_WIDTH, INPROJ_CHUNK, lambda y: y * jax.nn.sigmoid(y)),
        (gs_ref, D_MODEL, INPROJ_CHUNK, jax.nn.sigmoid),
        (gg_ref, D_MODEL, INPROJ_CHUNK, jax.nn.sigmoid),
    )
    feed, normed, plain, off = [], [], [], 0
    for ref, width, cw, post in segments:
        cw = min(cw, width)
        group = (normed if (ref is q_ref or ref is k_ref)
                 else feed if (ref is gq_s or ref is gk_s or ref is gv_s or ref is sr_s) else plain)
        group.extend((ref, lo, off + lo, cw, post) for lo in range(0, width, cw))
        off += width
    jobs = list(feed)
    while normed or plain:
        if plain:
            jobs.append(plain.pop(0))
        if normed:
            jobs.append(normed.pop(0))

    c = GLA_CHUNK
    cm = cm_ref[...]
    ri = lax.broadcasted_iota(jnp.int32, (c, c), 0)
    ci = lax.broadcasted_iota(jnp.int32, (c, c), 1)
    nt = (((1,), (1,)), ((), ()))
    tn = (((0,), (0,)), ((), ()))
    units = [(hh, slice(ch * c, (ch + 1) * c), slice(hh * GLA_DK, (hh + 1) * GLA_DK),
              slice(hh * GLA_DV, (hh + 1) * GLA_DV)) for hh in range(GLA_HEADS) for ch in range(tm // c)]
    n_units = len(units)
    level_masks = []
    for n in GLA_LEVELS:
        sh = n.bit_length() - 1
        level_masks.append(((ri >> sh) == (ci >> sh) + 1) & (((ri >> sh) & 1) == 1))

    args, o_intra, q_in = [None] * n_units, [None] * n_units, [None] * n_units
    decays, upds, states = [None] * n_units, [None] * n_units, [None] * GLA_HEADS

    def hi_lo(rows, kl):
        g = la_s[rows, kl]
        ghi = g.astype(BF16)
        glo = (g - ghi.astype(F32)).astype(BF16)
        return jnp.concatenate([ghi, glo], axis=0)

    def exponents(u):
        (_, r0, kl0, _), (_, r1, kl1, _) = units[u], units[u + 1]
        pair = jnp.dot(cm, jnp.concatenate([hi_lo(r0, kl0), hi_lo(r1, kl1)], axis=1),
                       preferred_element_type=F32)
        args[u], args[u + 1] = pair[:, :GLA_DK], pair[:, GLA_DK:]

    def within_chunk(u):
        _, rows, kl, vl = units[u]
        arg = args[u]
        ex = jnp.exp(arg.astype(BF16))
        qb = gq_s[rows, kl]
        kb = gk_s[rows, kl]
        vb = gv_s[rows, vl]
        sc = jnp.where(ri == ci, lax.dot_general(qb, kb, nt, preferred_element_type=F32), 0.0)
        for li in range(len(GLA_LEVELS)):
            ex_l = ex[(li + 2) * c:(li + 3) * c]
            s_l = lax.dot_general(qb * ex_l, kb * ex_l, nt, preferred_element_type=F32)
            sc = jnp.where(level_masks[li], s_l, sc)
        o_intra[u] = jnp.dot(sc.astype(BF16), vb, preferred_element_type=F32)
        q_in[u] = qb * ex[0:c]
        upds[u] = lax.dot_general(kb * ex[c:2 * c], vb, tn, preferred_element_type=F32)
        last = arg[c - 1:c, :]
        decay = jnp.broadcast_to(jnp.exp(last), (GLA_DK, GLA_DK)).T
        decays[u] = jnp.concatenate([decay, decay], axis=1)

    def across_chunks(u):
        hh, rows, _, vl = units[u]
        if states[hh] is None:
            states[hh] = state_ref[hh]
        state = states[hh]
        o = o_intra[u] + jnp.dot(q_in[u], state.astype(BF16), preferred_element_type=F32)
        states[hh] = state * decays[u] + upds[u]
        ms_o = jnp.mean(o * o, axis=-1, keepdims=True)
        y = o * lax.rsqrt(ms_o + RMS_EPS) * ng_ref[...]
        og_ref[rows, vl] = (y * sr_s[rows, vl].astype(F32)).astype(BF16)
        if u + 1 == n_units or units[u + 1][0] != hh:
            state_ref[hh] = states[hh]

    pieces = ([functools.partial(exponents, u) for u in range(0, n_units, 2)]
              + [functools.partial(within_chunk, u) for u in range(n_units)]
              + [functools.partial(across_chunks, u) for u in range(n_units)])
    n_rest = len(jobs) - len(feed)
    per_job = -(-len(pieces) // n_rest)

    lr = jnp.dot(h, wlr_ref[...], preferred_element_type=F32)
    y_next = proj(jobs[0][2], jobs[0][3])
    for n, (ref, lo, _, cw, post) in enumerate(jobs):
        y = y_next
        if n + 1 < len(jobs):
            y_next = proj(jobs[n + 1][2], jobs[n + 1][3])
        if n == 0:
            logit = jnp.dot(lr.astype(BF16), w2_ref[...], preferred_element_type=F32) + gb_ref[...]
            la_s[...] = _log_sigmoid(logit) * (1.0 / GLA_GATE_TAU)
        ref[:, lo:lo + cw] = post(y).astype(BF16)
        if n >= len(feed):
            for piece in pieces[:per_job]:
                piece()
            del pieces[:per_job]
    assert not pieces


def _in_projection_gla(x2, seq_len, g1, wa, wb, wlr, w2p, gb, qg, kg, avg, cm, ng):
    t = x2.shape[0]
    tm = INPROJ_ROW_TILE
    row = pl.BlockSpec((tm, D_MODEL), lambda i: (i, 0))
    bf = jax.ShapeDtypeStruct((t, D_MODEL), BF16)
    consts = (g1, wa, wb, wlr, w2p, gb, qg, kg, avg, cm, ng)
    return pl.pallas_call(
        functools.partial(_inproj_gla_kernel, steps_per_seq=seq_len // tm),
        grid=(t // tm,),
        in_specs=[row] + [_const_spec(c.shape) for c in consts],
        out_specs=(row,) * 6,
        out_shape=(bf,) * 6,
        scratch_shapes=[pltpu.VMEM((tm, GLA_KEY_WIDTH), BF16), pltpu.VMEM((tm, GLA_KEY_WIDTH), BF16),
                        pltpu.VMEM((tm, GLA_VALUE_WIDTH), BF16), pltpu.VMEM((tm, GLA_VALUE_WIDTH), BF16),
                        pltpu.VMEM((tm, GLA_KEY_WIDTH), F32),
                        pltpu.VMEM((GLA_HEADS, GLA_DK, GLA_DV), F32)],
        compiler_params=pltpu.CompilerParams(dimension_semantics=("arbitrary",),
                                             vmem_limit_bytes=VMEM_LIMIT_BYTES),
        name="in_projection_gla",
    )(x2, *consts)


def _sb_kernel(q_ref, k_ref, v_ref, mt_ref, o_ref, acc_ref, carry_ref):
    qi = pl.program_id(2)
    tq, tk, qt = q_ref.shape[1], SB_TK, SB_QT
    sub = tq // tk
    nt = (((1,), (1,)), ((), ()))
    tn = (((0,), (0,)), ((), ()))
    mt = mt_ref[...]
    one = jnp.ones((), BF16)
    zero16 = jnp.zeros((), BF16)
    first = qi * sub

    def head_split(q):
        lane = lax.broadcasted_iota(jnp.int32, q.shape, 1)
        zero = jnp.zeros_like(q)
        return (jnp.where(lane < SB_HEAD_DIM, q, zero), jnp.where(lane >= SB_HEAD_DIM, q, zero))

    def key_block(ref, j):
        return ref[0, pl.ds(pl.multiple_of(j * tk, tk), tk), :]

    def softplus2(zt):
        l2 = jnp.log2((one + jnp.exp2(-jnp.abs(zt))).astype(F32)).astype(BF16)
        return jnp.maximum(zt, zero16) + l2

    def causal(rows, width):
        return (lax.broadcasted_iota(jnp.int32, (rows, width), 0)
                < lax.broadcasted_iota(jnp.int32, (rows, width), 1))

    top_mask = causal(qt, tk)
    br_mask = causal(qt, qt)
    dead = jnp.zeros((qt, qt), BF16)

    def with_dead_quadrant(top, br):
        return jnp.concatenate([top, jnp.concatenate([dead, br], axis=1)], axis=0)

    def diag_logits(g):
        kb = key_block(k_ref, first + g)
        qh = head_split(q_ref[0, g * tk:(g + 1) * tk, :])
        return [lax.dot_general(kb, qh[h], nt, preferred_element_type=F32).astype(BF16)
                for h in range(2)]

    def diag_log_not(zts):
        out = []
        for zt in zts:
            p_top = jnp.where(top_mask, softplus2(zt[:qt, :]), zero16)
            p_br = jnp.where(br_mask, softplus2(zt[qt:, qt:]), zero16)
            cs = jnp.dot(mt, with_dead_quadrant(p_top, p_br), preferred_element_type=F32)
            out.append((p_top, p_br, cs))
        return out

    def diag_weights(g, zts, pcs):
        vb = key_block(v_ref, first + g)
        out = []
        for h in range(2):
            p_top, p_br, cs = pcs[h]
            zt = zts[h]
            w_top = jnp.exp2((zt[:qt, :] - p_top) + cs[:qt, :].astype(BF16))
            w_br = jnp.exp2((zt[qt:, qt:] - p_br) + cs[qt:, qt:].astype(BF16))
            w = with_dead_quadrant(jnp.where(top_mask, w_top, zero16), jnp.where(br_mask, w_br, zero16))
            pv = lax.dot_general(vb, w, tn, preferred_element_type=F32)
            out.append((pv, cs[0:1, :] - p_top[0:1, :].astype(F32)))
        return out

    mt_half = mt[:qt, :qt]

    def half_block(ref, j):
        return ref[0, pl.ds(pl.multiple_of(j * qt, qt), qt), :]

    def tile_logits(j, qh):
        kb = half_block(k_ref, j)
        return lax.dot_general(kb, jnp.concatenate(qh, axis=0), nt,
                               preferred_element_type=F32).astype(BF16)

    def tile_log_not(zt):
        p = softplus2(zt)
        return p, jnp.dot(mt_half, p, preferred_element_type=F32)

    def tile_weights(j, zt, pc, rows):
        vb = half_block(v_ref, j)
        p, cs = pc
        w = jnp.exp2(((zt - p) + cs.astype(BF16)) + jnp.concatenate(rows, axis=1).astype(BF16))
        pv = lax.dot_general(vb, w, tn, preferred_element_type=F32)
        tot = cs[0:1, :] - p[0:1, :].astype(F32)
        return [(pv[:, h * qt:(h + 1) * qt], tot[:, h * qt:(h + 1) * qt]) for h in range(2)]

    no_prev_bias = jnp.where(first > 0, 0.0, SB_NO_BLOCK_BIAS).astype(F32)
    halves_per_block = tk // qt
    prev_block = [jnp.maximum((first + g) * halves_per_block - 1, 0) for g in range(sub)]
    prev_q = [None] * sub
    n_units = 2 * sub
    zs, pcs, diag_out = {}, {}, {}

    def issue_logits(n):
        if n < sub:
            zs[n] = diag_logits(n)
        else:
            g = n - sub
            prev_q[g] = head_split(q_ref[0, g * tk:g * tk + qt, :])
            zs[n] = tile_logits(prev_block[g], prev_q[g])

    def issue_weights(n):
        if n < sub:
            diag_out[n] = diag_weights(n, zs.pop(n), pcs.pop(n))
            return
        g = n - sub
        rows = [diag_out[g][h][1][:, :qt] for h in range(2)]
        if g == 0:
            rows = [r + no_prev_bias for r in rows]
        res = tile_weights(prev_block[g], zs.pop(n), pcs.pop(n), rows)
        for h in range(2):
            pv_d, tot_d = diag_out[g][h]
            pv_p, tot_p = res[h]
            acc_ref[h, 2 * g] = pv_d[:, :qt] + pv_p
            acc_ref[h, 2 * g + 1] = pv_d[:, qt:]
            carry_ref[h, 2 * g:2 * g + 1, :] = tot_d[:, :qt] + tot_p
            carry_ref[h, 2 * g + 1:2 * g + 2, :] = tot_d[:, qt:]

    issue_logits(0)
    for n in range(n_units):
        if n + 1 < n_units:
            issue_logits(n + 1)
        pcs[n] = diag_log_not(zs[n]) if n < sub else tile_log_not(zs[n])
        if n >= 1:
            issue_weights(n - 1)
    issue_weights(n_units - 1)

    n_tiles = tq // qt
    tile_ids = lax.broadcasted_iota(jnp.int32, (n_tiles, 1), 0)

    def first_unvisited(c):
        return (first + (c >> 1)) * halves_per_block - 2 + (c & 1)

    next_block = first_unvisited(tile_ids)

    def next_live_tile(after):
        top = jnp.max(jnp.maximum(carry_ref[0], carry_ref[1]), axis=1, keepdims=True)
        todo = (top >= SB_UNDERFLOW_LOG2) & (next_block >= 0) & (tile_ids > after)
        return jnp.min(jnp.where(todo, tile_ids, n_tiles))

    def walk_tile(c):
        qh = head_split(q_ref[0, pl.ds(pl.multiple_of(c * qt, qt), qt), :])

        def live(state):
            j, top = state
            return jnp.logical_and(j >= 0, top >= SB_UNDERFLOW_LOG2)

        def step(state):
            j, _ = state
            zts = tile_logits(j, qh)
            rows = [carry_ref[h, pl.ds(c, 1), :] for h in range(2)]
            res = tile_weights(j, zts, tile_log_not(zts), rows)
            new_rows = [rows[h] + res[h][1] for h in range(2)]
            for h in range(2):
                acc_ref[h, c] += res[h][0]
                carry_ref[h, pl.ds(c, 1), :] = new_rows[h]
            return j - 1, jnp.max(jnp.maximum(new_rows[0], new_rows[1]))

        lax.while_loop(live, step, (first_unvisited(c), jnp.float32(0.0)))
        return next_live_tile(c)

    lax.while_loop(lambda c: c < n_tiles, walk_tile, next_live_tile(-1))

    row = lax.broadcasted_iota(jnp.int32, (LANES, qt), 0)
    for c in range(tq // qt):
        ot = jnp.where(row < SB_HEAD_DIM, acc_ref[0, c], acc_ref[1, c])
        o_ref[0, c * qt:(c + 1) * qt, :] = ot.T.astype(BF16)


def _sb_attention(q, k, v, mt):
    b, s, _ = q.shape
    tq = min(SB_TQ, s)
    qspec = pl.BlockSpec((1, tq, LANES), lambda bi, hp, i: (bi, i, hp))
    kvspec = pl.BlockSpec((1, s, LANES), lambda bi, hp, i: (bi, 0, hp))
    return pl.pallas_call(
        _sb_kernel,
        grid=(b, SB_WIDTH // LANES, s // tq),
        in_specs=[qspec, kvspec, kvspec, _const_spec(mt.shape)],
        out_specs=qspec,
        out_shape=jax.ShapeDtypeStruct((b, s, SB_WIDTH), BF16),
        scratch_shapes=[pltpu.VMEM((2, tq // SB_QT, LANES, SB_QT), F32),
                        pltpu.VMEM((2, tq // SB_QT, SB_QT), F32)],
        compiler_params=pltpu.CompilerParams(
            dimension_semantics=("arbitrary", "arbitrary", "arbitrary"),
            vmem_limit_bytes=VMEM_LIMIT_BYTES),
        name="stick_breaking_attention",
    )(q, k, v, mt)


def _gla_pivot_rows(n):
    rows = np.arange(GLA_CHUNK)
    base = (rows // n) * n
    second = (rows // n) % 2 == 1
    return second, np.where(second, base - 1, base + n - 1)


def _gla_level_matrix():
    c = GLA_CHUNK
    r = np.arange(c)[:, None]
    m = np.arange(c)[None, :]
    mats = [m <= r, m > r]
    for n in GLA_LEVELS:
        _, idx = _gla_pivot_rows(n)
        lo = np.minimum(r[:, 0], idx)[:, None]
        hi = np.maximum(r[:, 0], idx)[:, None]
        mats.append((m > lo) & (m <= hi))
    one = np.concatenate(mats, axis=0).astype(np.float32)
    return np.concatenate([one, one], axis=1)


def _tail_kernel(x_ref, osb_ref, ogla_ref, gs_ref, gg_ref, wsb_ref, wgla_ref, wout_ref,
                 g2_ref, wg_ref, wu_ref, wd_ref, o_ref):
    tm = x_ref.shape[0]
    halves = [slice(0, tm // 2), slice(tm // 2, tm)]
    dot = functools.partial(jnp.dot, preferred_element_type=F32)
    ab = [(dot(osb_ref[sl, :], wsb_ref[...]), dot(ogla_ref[sl, :], wgla_ref[...])) for sl in halves]
    merged = [(gs_ref[sl, :].astype(F32) * a + gg_ref[sl, :].astype(F32) * b).astype(BF16)
              for sl, (a, b) in zip(halves, ab)]
    x1 = [x_ref[sl, :] + dot(m, wout_ref[...]) for sl, m in zip(halves, merged)]
    h2 = []
    for x1h in x1:
        ms = jnp.mean(x1h * x1h, axis=-1, keepdims=True)
        h2.append((x1h * lax.rsqrt(ms + RMS_EPS) * g2_ref[...]).astype(BF16))
    gu = [(dot(h, wg_ref[...]), dot(h, wu_ref[...])) for h in h2]
    hid = [(gate * jax.nn.sigmoid(gate) * up).astype(BF16) for gate, up in gu]
    for sl, x1h, hd in zip(halves, x1, hid):
        o_ref[sl, :] = x1h + dot(hd, wd_ref[...])


def _tail(x2, osb, ogla, gs, gg, wsb, wgla, wout, g2, wg, wu, wd):
    t = x2.shape[0]
    tm = ROW_TILE
    row = pl.BlockSpec((tm, D_MODEL), lambda i: (i, 0))
    consts = (wsb, wgla, wout, g2, wg, wu, wd)
    return pl.pallas_call(
        _tail_kernel,
        grid=(t // tm,),
        in_specs=[row] * 5 + [_const_spec(c.shape) for c in consts],
        out_specs=row,
        out_shape=jax.ShapeDtypeStruct((t, D_MODEL), F32),
        compiler_params=pltpu.CompilerParams(dimension_semantics=("arbitrary",),
                                             vmem_limit_bytes=VMEM_LIMIT_BYTES),
        name="merge_out_ffn",
    )(x2, osb, ogla, gs, gg, *consts)


def _layer(x, norm1_g, w_in, sb_q_norm_g, sb_k_norm_g, gla_gate_w2, gla_gate_b, gla_out_norm_g,
           w_branch_sb, w_branch_gla, w_out, norm2_g, w_ffn_gate, w_ffn_up, w_ffn_down):
    b, s, d = x.shape
    t = b * s
    x2 = x.reshape(t, d)

    lr0 = 3 * SB_WIDTH + 2 * GLA_KEY_WIDTH + 2 * GLA_VALUE_WIDTH
    lr1 = lr0 + GLA_GATE_RANK
    wa = w_in[:, :lr0].astype(BF16)
    wb = w_in[:, lr1:].astype(BF16)
    wlr = jnp.pad(w_in[:, lr0:lr1], ((0, 0), (0, LANES - GLA_GATE_RANK))).astype(BF16)
    w2p = jnp.pad(gla_gate_w2, ((0, LANES - GLA_GATE_RANK), (0, 0))).astype(BF16)

    heads_per_tile = MXU_DIM // SB_HEAD_DIM
    qg = jnp.tile(sb_q_norm_g * (SB_HEAD_DIM ** -0.5 * LOG2E), heads_per_tile).reshape(1, MXU_DIM)
    kg = jnp.tile(sb_k_norm_g, heads_per_tile).reshape(1, MXU_DIM)
    hid = np.arange(MXU_DIM) // SB_HEAD_DIM
    avg = jnp.asarray((hid[:, None] == hid[None, :]).astype(np.float32) / SB_HEAD_DIM, BF16)

    cm = jnp.asarray(_gla_level_matrix(), BF16)
    q, k, v, gs, gg, o_gla = _in_projection_gla(
        x2, s, norm1_g.reshape(1, d), wa, wb, wlr, w2p, gla_gate_b.reshape(1, -1), qg, kg, avg,
        cm, gla_out_norm_g.reshape(1, -1))

    idx = np.arange(SB_TK)
    mt = jnp.asarray(-(idx[None, :] > idx[:, None]).astype(np.float32), BF16)
    o_sb = _sb_attention(q.reshape(b, s, -1), k.reshape(b, s, -1), v.reshape(b, s, -1), mt)

    out = _tail(x2, o_sb.reshape(t, -1), o_gla, gs, gg,
                w_branch_sb.astype(BF16), w_branch_gla.astype(BF16), w_out.astype(BF16),
                norm2_g.reshape(1, d), w_ffn_gate.astype(BF16), w_ffn_up.astype(BF16),
                w_ffn_down.astype(BF16))
    return out.reshape(b, s, d)


def kernel(x, norm1_g, w_in, sb_q_norm_g, sb_k_norm_g, gla_gate_w2, gla_gate_b, gla_out_norm_g,
           w_branch_sb, w_branch_gla, w_out, norm2_g, w_ffn_gate, w_ffn_up, w_ffn_down):
    for l in range(norm1_g.shape[0]):
        x = _layer(x, norm1_g[l], w_in[l], sb_q_norm_g[l], sb_k_norm_g[l], gla_gate_w2[l],
                   gla_gate_b[l], gla_out_norm_g[l], w_branch_sb[l], w_branch_gla[l], w_out[l],
                   norm2_g[l], w_ffn_gate[l], w_ffn_up[l], w_ffn_down[l])
    return x
```

```python
import functools

import jax
import jax.numpy as jnp
import numpy as np
from jax import lax
from jax.experimental import pallas as pl
from jax.experimental.pallas import tpu as pltpu

F32 = jnp.float32
BF16 = jnp.bfloat16

D_MODEL = 1024
SB_HEADS = 16
SB_HEAD_DIM = 64
SB_WIDTH = SB_HEADS * SB_HEAD_DIM
GLA_HEADS = 4
GLA_KEY_WIDTH = D_MODEL // 2
GLA_VALUE_WIDTH = D_MODEL
GLA_DK = GLA_KEY_WIDTH // GLA_HEADS
GLA_DV = GLA_VALUE_WIDTH // GLA_HEADS
GLA_GATE_RANK = 16
GLA_GATE_TAU = 16.0
D_FF = -(-8 * D_MODEL // (3 * 256)) * 256
RMS_EPS = 1e-6
LOG2E = 1.4426950408889634

LANES = 128
MXU_DIM = 256
VMEM_LIMIT_BYTES = 56 * 1024 * 1024

ROW_TILE = 512
INPROJ_ROW_TILE = 512
INPROJ_CHUNK = 512
SB_TQ = 4096
SB_TK = 256
SB_QT = 128
SB_UNDERFLOW_LOG2 = -126.0
SB_NO_BLOCK_BIAS = -1e30
GLA_CHUNK = 64
GLA_VECTOR_LEVELS = (32, 16, 8)
GLA_MATMUL_LEVELS = (4, 2, 1)
GLA_LEVELS = GLA_VECTOR_LEVELS + GLA_MATMUL_LEVELS


def _const_spec(shape):
    nd = len(shape)
    return pl.BlockSpec(shape, lambda *_: (0,) * nd, pipeline_mode=pl.Buffered(1))


def _log_sigmoid(x):
    return jnp.minimum(x, 0.0) - jnp.log(1.0 + jnp.exp(-jnp.abs(x)))


def _inproj_gla_kernel(x_ref, g1_ref, w_ref, w2_ref, gb_ref, qg_ref, kg_ref, avg_ref,
                       cm_ref, ng_ref,
                       q_ref, k_ref, v_ref, gs_ref, gg_ref, og_ref,
                       wb_s, gq_s, gk_s, gv_s, sr_s, la_s, state_ref, *, steps_per_seq):
    tm = x_ref.shape[0]

    @pl.when(pl.program_id(0) % steps_per_seq == 0)
    def _():
        state_ref[...] = jnp.zeros_like(state_ref)

    x = x_ref[...]
    ms = jnp.mean(x * x, axis=-1, keepdims=True)
    h = (x * lax.rsqrt(ms + RMS_EPS) * g1_ref[...]).astype(BF16)
    rank0 = 3 * SB_WIDTH + 2 * GLA_KEY_WIDTH + 2 * GLA_VALUE_WIDTH
    rank1 = rank0 + GLA_GATE_RANK

    @pl.when(pl.program_id(0) == 0)
    def _():
        wb_s[...] = w_ref[:, rank1:]

    def proj(c0, cw):
        w = w_ref[:, c0:c0 + cw] if c0 < rank0 else wb_s[:, c0 - rank0:c0 - rank0 + cw]
        return jnp.dot(h, w, preferred_element_type=F32)

    def head_norm(y, gain):
        m = jnp.dot((y * y).astype(BF16), avg_ref[...], preferred_element_type=F32)
        return y * lax.rsqrt(m + RMS_EPS) * gain

    segments = (
        (q_ref, SB_WIDTH, MXU_DIM, lambda y: head_norm(y, qg_ref[...])),
        (k_ref, SB_WIDTH, MXU_DIM, lambda y: head_norm(y, kg_ref[...])),
        (v_ref, SB_WIDTH, INPROJ_CHUNK, lambda y: y),
        (gq_s, GLA_KEY_WIDTH, INPROJ_CHUNK, lambda y: y * (GLA_DK ** -0.5)),
        (gk_s, GLA_KEY_WIDTH, INPROJ_CHUNK, lambda y: y),
        (gv_s, GLA_VALUE_WIDTH, INPROJ_CHUNK, lambda y: y),
        (sr_s, GLA_VALUE_WIDTH, INPROJ_CHUNK, lambda y: y * jax.nn.sigmoid(y)),
        (gs_ref, D_MODEL, INPROJ_CHUNK, jax.nn.sigmoid),
        (gg_ref, D_MODEL, INPROJ_CHUNK, jax.nn.sigmoid),
    )
    feed, normed, plain, off = [], [], [], 0
    for ref, width, cw, post in segments:
        cw = min(cw, width)
        group = (normed if (ref is q_ref or ref is k_ref)
                 else feed if (ref is gq_s or ref is gk_s or ref is gv_s or ref is sr_s) else plain)
        group.extend((ref, lo, off + lo, cw, post) for lo in range(0, width, cw))
        off += width
    jobs = list(feed)
    while normed or plain:
        if plain:
            jobs.append(plain.pop(0))
        if normed:
            jobs.append(normed.pop(0))

    c = GLA_CHUNK
    cm = cm_ref[...]
    ri = lax.broadcasted_iota(jnp.int32, (c, c), 0)
    ci = lax.broadcasted_iota(jnp.int32, (c, c), 1)
    nt = (((1,), (1,)), ((), ()))
    tn = (((0,), (0,)), ((), ()))
    units = [(hh, slice(ch * c, (ch + 1) * c), slice(hh * GLA_DK, (hh + 1) * GLA_DK),
              slice(hh * GLA_DV, (hh + 1) * GLA_DV)) for hh in range(GLA_HEADS) for ch in range(tm // c)]
    n_units = len(units)
    level_masks = []
    for n in GLA_LEVELS:
        sh = n.bit_length() - 1
        level_masks.append(((ri >> sh) == (ci >> sh) + 1) & (((ri >> sh) & 1) == 1))

    args, o_intra, q_in = [None] * n_units, [None] * n_units, [None] * n_units
    decays, upds, states = [None] * n_units, [None] * n_units, [None] * GLA_HEADS

    def hi_lo(rows, kl):
        g = la_s[rows, kl]
        ghi = g.astype(BF16)
        glo = (g - ghi.astype(F32)).astype(BF16)
        return jnp.concatenate([ghi, glo], axis=0)

    def exponents(u):
        (_, r0, kl0, _), (_, r1, kl1, _) = units[u], units[u + 1]
        pair = jnp.dot(cm, jnp.concatenate([hi_lo(r0, kl0), hi_lo(r1, kl1)], axis=1),
                       preferred_element_type=F32)
        for i in range(2):
            part = pair[:, i * GLA_DK:(i + 1) * GLA_DK]
            cum = part[0:c]
            blocks = [cum, cum[c - 1:c, :] - cum]
            for n in GLA_VECTOR_LEVELS:
                for b0 in range(0, c, 2 * n):
                    pivot = cum[b0 + n - 1:b0 + n, :]
                    blocks += [pivot - cum[b0:b0 + n, :], cum[b0 + n:b0 + 2 * n, :] - pivot]
            args[u + i] = jnp.concatenate(blocks + [part[c:]], axis=0)

    def within_chunk(u):
        _, rows, kl, vl = units[u]
        arg = args[u]
        ex = jnp.exp(arg.astype(BF16))
        qb = gq_s[rows, kl]
        kb = gk_s[rows, kl]
        vb = gv_s[rows, vl]
        sc = jnp.where(ri == ci, lax.dot_general(qb, kb, nt, preferred_element_type=F32), 0.0)
        for li in range(len(GLA_LEVELS)):
            ex_l = ex[(li + 2) * c:(li + 3) * c]
            s_l = lax.dot_general(qb * ex_l, kb * ex_l, nt, preferred_element_type=F32)
            sc = jnp.where(level_masks[li], s_l, sc)
        o_intra[u] = jnp.dot(sc.astype(BF16), vb, preferred_element_type=F32)
        q_in[u] = qb * ex[0:c]
        upds[u] = lax.dot_general(kb * ex[c:2 * c], vb, tn, preferred_element_type=F32)
        last = arg[c - 1:c, :]
        decay = jnp.broadcast_to(jnp.exp(last), (GLA_DK, GLA_DK)).T
        decays[u] = jnp.concatenate([decay, decay], axis=1)

    def across_chunks(u):
        hh, rows, _, vl = units[u]
        if states[hh] is None:
            states[hh] = state_ref[hh]
        state = states[hh]
        o = o_intra[u] + jnp.dot(q_in[u], state.astype(BF16), preferred_element_type=F32)
        states[hh] = state * decays[u] + upds[u]
        ms_o = jnp.mean(o * o, axis=-1, keepdims=True)
        y = o * lax.rsqrt(ms_o + RMS_EPS) * ng_ref[...]
        og_ref[rows, vl] = (y * sr_s[rows, vl].astype(F32)).astype(BF16)
        if u + 1 == n_units or units[u + 1][0] != hh:
            state_ref[hh] = states[hh]

    pieces = ([functools.partial(exponents, u) for u in range(0, n_units, 2)]
              + [functools.partial(within_chunk, u) for u in range(n_units)]
              + [functools.partial(across_chunks, u) for u in range(n_units)])
    n_rest = len(jobs) - len(feed)
    per_job = -(-len(pieces) // n_rest)

    lr = jnp.dot(h, w_ref[:, rank0:rank0 + LANES], preferred_element_type=F32)
    y_next = proj(jobs[0][2], jobs[0][3])
    for n, (ref, lo, _, cw, post) in enumerate(jobs):
        y = y_next
        if n + 1 < len(jobs):
            y_next = proj(jobs[n + 1][2], jobs[n + 1][3])
        if n == 0:
            logit = jnp.dot(lr.astype(BF16), w2_ref[...], preferred_element_type=F32) + gb_ref[...]
            la_s[...] = _log_sigmoid(logit) * (1.0 / GLA_GATE_TAU)
        ref[:, lo:lo + cw] = post(y).astype(BF16)
        if n >= len(feed):
            for piece in pieces[:per_job]:
                piece()
            del pieces[:per_job]
    assert not pieces


def _in_projection_gla(x2, seq_len, g1, w, w2p, gb, qg, kg, avg, cm, ng):
    t = x2.shape[0]
    tm = INPROJ_ROW_TILE
    row = pl.BlockSpec((tm, D_MODEL), lambda i: (i, 0))
    bf = jax.ShapeDtypeStruct((t, D_MODEL), BF16)
    consts = (g1, w, w2p, gb, qg, kg, avg, cm, ng)
    return pl.pallas_call(
        functools.partial(_inproj_gla_kernel, steps_per_seq=seq_len // tm),
        grid=(t // tm,),
        in_specs=[row] + [_const_spec(c.shape) for c in consts],
        out_specs=(row,) * 6,
        out_shape=(bf,) * 6,
        scratch_shapes=[pltpu.VMEM((D_MODEL, 2 * D_MODEL), BF16),
                        pltpu.VMEM((tm, GLA_KEY_WIDTH), BF16), pltpu.VMEM((tm, GLA_KEY_WIDTH), BF16),
                        pltpu.VMEM((tm, GLA_VALUE_WIDTH), BF16), pltpu.VMEM((tm, GLA_VALUE_WIDTH), BF16),
                        pltpu.VMEM((tm, GLA_KEY_WIDTH), F32),
                        pltpu.VMEM((GLA_HEADS, GLA_DK, GLA_DV), F32)],
        compiler_params=pltpu.CompilerParams(dimension_semantics=("arbitrary",),
                                             vmem_limit_bytes=VMEM_LIMIT_BYTES),
        name="in_projection_gla",
    )(x2, *consts)


def _sb_kernel(q_ref, k_ref, v_ref, mt_ref, o_ref, acc_ref, carry_ref):
    qi = pl.program_id(2)
    tq, tk, qt = q_ref.shape[1], SB_TK, SB_QT
    sub = tq // tk
    nt = (((1,), (1,)), ((), ()))
    tn = (((0,), (0,)), ((), ()))
    mt = mt_ref[...]
    one = jnp.ones((), BF16)
    zero16 = jnp.zeros((), BF16)
    first = qi * sub

    def head_split(q):
        lane = lax.broadcasted_iota(jnp.int32, q.shape, 1)
        zero = jnp.zeros_like(q)
        return (jnp.where(lane < SB_HEAD_DIM, q, zero), jnp.where(lane >= SB_HEAD_DIM, q, zero))

    def key_block(ref, j):
        return ref[0, pl.ds(pl.multiple_of(j * tk, tk), tk), :]

    def softplus2(zt):
        l2 = jnp.log2((one + jnp.exp2(-jnp.abs(zt))).astype(F32)).astype(BF16)
        return jnp.maximum(zt, zero16) + l2

    def causal(rows, width):
        return (lax.broadcasted_iota(jnp.int32, (rows, width), 0)
                < lax.broadcasted_iota(jnp.int32, (rows, width), 1))

    top_mask = causal(qt, tk)
    br_mask = causal(qt, qt)
    dead = jnp.zeros((qt, qt), BF16)

    def with_dead_quadrant(top, br):
        return jnp.concatenate([top, jnp.concatenate([dead, br], axis=1)], axis=0)

    def diag_logits(g):
        kb = key_block(k_ref, first + g)
        qh = head_split(q_ref[0, g * tk:(g + 1) * tk, :])
        return [lax.dot_general(kb, qh[h], nt, preferred_element_type=F32).astype(BF16)
                for h in range(2)]

    def diag_log_not(zts):
        out = []
        for zt in zts:
            p_top = jnp.where(top_mask, softplus2(zt[:qt, :]), zero16)
            p_br = jnp.where(br_mask, softplus2(zt[qt:, qt:]), zero16)
            cs = jnp.dot(mt, with_dead_quadrant(p_top, p_br), preferred_element_type=F32)
            out.append((p_top, p_br, cs))
        return out

    def diag_weights(g, zts, pcs):
        vb = key_block(v_ref, first + g)
        out = []
        for h in range(2):
            p_top, p_br, cs = pcs[h]
            zt = zts[h]
            w_top = jnp.exp2((zt[:qt, :] - p_top) + cs[:qt, :].astype(BF16))
            w_br = jnp.exp2((zt[qt:, qt:] - p_br) + cs[qt:, qt:].astype(BF16))
            w = with_dead_quadrant(jnp.where(top_mask, w_top, zero16), jnp.where(br_mask, w_br, zero16))
            pv = lax.dot_general(vb, w, tn, preferred_element_type=F32)
            out.append((pv, cs[0:1, :] - p_top[0:1, :].astype(F32)))
        return out

    mt_half = mt[:qt, :qt]

    def half_block(ref, j):
        return ref[0, pl.ds(pl.multiple_of(j * qt, qt), qt), :]

    def tile_logits(j, qh):
        kb = half_block(k_ref, j)
        return lax.dot_general(kb, jnp.concatenate(qh, axis=0), nt,
                               preferred_element_type=F32).astype(BF16)

    def tile_log_not(zt):
        p = softplus2(zt)
        return p, jnp.dot(mt_half, p, preferred_element_type=F32)

    def tile_weights(j, zt, pc, rows):
        vb = half_block(v_ref, j)
        p, cs = pc
        w = jnp.exp2(((zt - p) + cs.astype(BF16)) + jnp.concatenate(rows, axis=1).astype(BF16))
        pv = lax.dot_general(vb, w, tn, preferred_element_type=F32)
        tot = cs[0:1, :] - p[0:1, :].astype(F32)
        return [(pv[:, h * qt:(h + 1) * qt], tot[:, h * qt:(h + 1) * qt]) for h in range(2)]

    no_prev_bias = jnp.where(first > 0, 0.0, SB_NO_BLOCK_BIAS).astype(F32)
    halves_per_block = tk // qt
    prev_block = [jnp.maximum((first + g) * halves_per_block - 1, 0) for g in range(sub)]
    prev_q = [None] * sub
    n_units = 2 * sub
    zs, pcs, diag_out = {}, {}, {}

    def issue_logits(n):
        if n < sub:
            zs[n] = diag_logits(n)
        else:
            g = n - sub
            prev_q[g] = head_split(q_ref[0, g * tk:g * tk + qt, :])
            zs[n] = tile_logits(prev_block[g], prev_q[g])

    def issue_weights(n):
        if n < sub:
            diag_out[n] = diag_weights(n, zs.pop(n), pcs.pop(n))
            return
        g = n - sub
        rows = [diag_out[g][h][1][:, :qt] for h in range(2)]
        if g == 0:
            rows = [r + no_prev_bias for r in rows]
        res = tile_weights(prev_block[g], zs.pop(n), pcs.pop(n), rows)
        for h in range(2):
            pv_d, tot_d = diag_out[g][h]
            pv_p, tot_p = res[h]
            acc_ref[h, 2 * g] = pv_d[:, :qt] + pv_p
            acc_ref[h, 2 * g + 1] = pv_d[:, qt:]
            carry_ref[h, 2 * g:2 * g + 1, :] = tot_d[:, :qt] + tot_p
            carry_ref[h, 2 * g + 1:2 * g + 2, :] = tot_d[:, qt:]

    issue_logits(0)
    for n in range(n_units):
        if n + 1 < n_units:
            issue_logits(n + 1)
        pcs[n] = diag_log_not(zs[n]) if n < sub else tile_log_not(zs[n])
        if n >= 1:
            issue_weights(n - 1)
    issue_weights(n_units - 1)

    n_tiles = tq // qt
    tile_ids = lax.broadcasted_iota(jnp.int32, (n_tiles, 1), 0)

    def first_unvisited(c):
        return (first + (c >> 1)) * halves_per_block - 2 + (c & 1)

    next_block = first_unvisited(tile_ids)

    def next_live_tile(after):
        top = jnp.max(jnp.maximum(carry_ref[0], carry_ref[1]), axis=1, keepdims=True)
        todo = (top >= SB_UNDERFLOW_LOG2) & (next_block >= 0) & (tile_ids > after)
        return jnp.min(jnp.where(todo, tile_ids, n_tiles))

    def walk_tile(c):
        qh = head_split(q_ref[0, pl.ds(pl.multiple_of(c * qt, qt), qt), :])

        def live(state):
            j, top = state
            return jnp.logical_and(j >= 0, top >= SB_UNDERFLOW_LOG2)

        def step(state):
            j, _ = state
            zts = tile_logits(j, qh)
            rows = [carry_ref[h, pl.ds(c, 1), :] for h in range(2)]
            res = tile_weights(j, zts, tile_log_not(zts), rows)
            new_rows = [rows[h] + res[h][1] for h in range(2)]
            for h in range(2):
                acc_ref[h, c] += res[h][0]
                carry_ref[h, pl.ds(c, 1), :] = new_rows[h]
            return j - 1, jnp.max(jnp.maximum(new_rows[0], new_rows[1]))

        lax.while_loop(live, step, (first_unvisited(c), jnp.float32(0.0)))
        return next_live_tile(c)

    lax.while_loop(lambda c: c < n_tiles, walk_tile, next_live_tile(-1))

    row = lax.broadcasted_iota(jnp.int32, (LANES, qt), 0)
    for c in range(tq // qt):
        ot = jnp.where(row < SB_HEAD_DIM, acc_ref[0, c], acc_ref[1, c])
        o_ref[0, c * qt:(c + 1) * qt, :] = ot.T.astype(BF16)


def _sb_attention(q, k, v, mt):
    b, s, _ = q.shape
    tq = min(SB_TQ, s)
    qspec = pl.BlockSpec((1, tq, LANES), lambda bi, hp, i: (bi, i, hp))
    kvspec = pl.BlockSpec((1, s, LANES), lambda bi, hp, i: (bi, 0, hp))
    return pl.pallas_call(
        _sb_kernel,
        grid=(b, SB_WIDTH // LANES, s // tq),
        in_specs=[qspec, kvspec, kvspec, _const_spec(mt.shape)],
        out_specs=qspec,
        out_shape=jax.ShapeDtypeStruct((b, s, SB_WIDTH), BF16),
        scratch_shapes=[pltpu.VMEM((2, tq // SB_QT, LANES, SB_QT), F32),
                        pltpu.VMEM((2, tq // SB_QT, SB_QT), F32)],
        compiler_params=pltpu.CompilerParams(
            dimension_semantics=("arbitrary", "arbitrary", "arbitrary"),
            vmem_limit_bytes=VMEM_LIMIT_BYTES),
        name="stick_breaking_attention",
    )(q, k, v, mt)


def _gla_pivot_rows(n):
    rows = np.arange(GLA_CHUNK)
    base = (rows // n) * n
    second = (rows // n) % 2 == 1
    return second, np.where(second, base - 1, base + n - 1)


def _gla_level_matrix():
    c = GLA_CHUNK
    r = np.arange(c)[:, None]
    m = np.arange(c)[None, :]
    mats = [m <= r]
    for n in GLA_MATMUL_LEVELS:
        _, idx = _gla_pivot_rows(n)
        lo = np.minimum(r[:, 0], idx)[:, None]
        hi = np.maximum(r[:, 0], idx)[:, None]
        mats.append((m > lo) & (m <= hi))
    one = np.concatenate(mats, axis=0).astype(np.float32)
    return np.concatenate([one, one], axis=1)


def _tail_kernel(x_ref, osb_ref, ogla_ref, gs_ref, gg_ref, wsb_ref, wgla_ref, wout_ref,
                 g2_ref, wg_ref, wu_ref, wd_ref, o_ref):
    tm = x_ref.shape[0]
    halves = [slice(0, tm // 2), slice(tm // 2, tm)]
    dot = functools.partial(jnp.dot, preferred_element_type=F32)
    ab = [(dot(osb_ref[sl, :], wsb_ref[...]), dot(ogla_ref[sl, :], wgla_ref[...])) for sl in halves]
    merged = [(gs_ref[sl, :].astype(F32) * a + gg_ref[sl, :].astype(F32) * b).astype(BF16)
              for sl, (a, b) in zip(halves, ab)]
    x1 = [x_ref[sl, :] + dot(m, wout_ref[...]) for sl, m in zip(halves, merged)]
    h2 = []
    for x1h in x1:
        ms = jnp.mean(x1h * x1h, axis=-1, keepdims=True)
        h2.append((x1h * lax.rsqrt(ms + RMS_EPS) * g2_ref[...]).astype(BF16))
    gu = [(dot(h, wg_ref[...]), dot(h, wu_ref[...])) for h in h2]
    hid = [(gate * jax.nn.sigmoid(gate) * up).astype(BF16) for gate, up in gu]
    for sl, x1h, hd in zip(halves, x1, hid):
        o_ref[sl, :] = x1h + dot(hd, wd_ref[...])


def _tail(x2, osb, ogla, gs, gg, wsb, wgla, wout, g2, wg, wu, wd):
    t = x2.shape[0]
    tm = ROW_TILE
    row = pl.BlockSpec((tm, D_MODEL), lambda i: (i, 0))
    consts = (wsb, wgla, wout, g2, wg, wu, wd)
    return pl.pallas_call(
        _tail_kernel,
        grid=(t // tm,),
        in_specs=[row] * 5 + [_const_spec(c.shape) for c in consts],
        out_specs=row,
        out_shape=jax.ShapeDtypeStruct((t, D_MODEL), F32),
        compiler_params=pltpu.CompilerParams(dimension_semantics=("arbitrary",),
                                             vmem_limit_bytes=VMEM_LIMIT_BYTES),
        name="merge_out_ffn",
    )(x2, osb, ogla, gs, gg, *consts)


def _layer(x, norm1_g, w_in, sb_q_norm_g, sb_k_norm_g, gla_gate_w2, gla_gate_b, gla_out_norm_g,
           w_branch_sb, w_branch_gla, w_out, norm2_g, w_ffn_gate, w_ffn_up, w_ffn_down):
    b, s, d = x.shape
    t = b * s
    x2 = x.reshape(t, d)

    w2p = jnp.pad(gla_gate_w2, ((0, LANES - GLA_GATE_RANK), (0, 0))).astype(BF16)

    heads_per_tile = MXU_DIM // SB_HEAD_DIM
    qg = jnp.tile(sb_q_norm_g * (SB_HEAD_DIM ** -0.5 * LOG2E), heads_per_tile).reshape(1, MXU_DIM)
    kg = jnp.tile(sb_k_norm_g, heads_per_tile).reshape(1, MXU_DIM)
    hid = np.arange(MXU_DIM) // SB_HEAD_DIM
    avg = jnp.asarray((hid[:, None] == hid[None, :]).astype(np.float32) / SB_HEAD_DIM, BF16)

    cm = jnp.asarray(_gla_level_matrix(), BF16)
    q, k, v, gs, gg, o_gla = _in_projection_gla(
        x2, s, norm1_g.reshape(1, d), w_in.astype(BF16), w2p, gla_gate_b.reshape(1, -1), qg, kg, avg,
        cm, gla_out_norm_g.reshape(1, -1))

    idx = np.arange(SB_TK)
    mt = jnp.asarray(-(idx[None, :] > idx[:, None]).astype(np.float32), BF16)
    o_sb = _sb_attention(q.reshape(b, s, -1), k.reshape(b, s, -1), v.reshape(b, s, -1), mt)

    out = _tail(x2, o_sb.reshape(t, -1), o_gla, gs, gg,
                w_branch_sb.astype(BF16), w_branch_gla.astype(BF16), w_out.astype(BF16),
                norm2_g.reshape(1, d), w_ffn_gate.astype(BF16), w_ffn_up.astype(BF16),
                w_ffn_down.astype(BF16))
    return out.reshape(b, s, d)


def kernel(x, norm1_g, w_in, sb_q_norm_g, sb_k_norm_g, gla_gate_w2, gla_gate_b, gla_out_norm_g,
           w_branch_sb, w_branch_gla, w_out, norm2_g, w_ffn_gate, w_ffn_up, w_ffn_down):
    for l in range(norm1_g.shape[0]):
        x = _layer(x, norm1_g[l], w_in[l], sb_q_norm_g[l], sb_k_norm_g[l], gla_gate_w2[l],
                   gla_gate_b[l], gla_out_norm_g[l], w_branch_sb[l], w_branch_gla[l], w_out[l],
                   norm2_g[l], w_ffn_gate[l], w_ffn_up[l], w_ffn_down[l])
    return x
```

```python
import functools

import jax
import jax.numpy as jnp
import numpy as np
from jax import lax
from jax.experimental import pallas as pl
from jax.experimental.pallas import tpu as pltpu

F32 = jnp.float32
BF16 = jnp.bfloat16

D_MODEL = 1024
SB_HEADS = 16
SB_HEAD_DIM = 64
SB_WIDTH = SB_HEADS * SB_HEAD_DIM
GLA_HEADS = 4
GLA_KEY_WIDTH = D_MODEL // 2
GLA_VALUE_WIDTH = D_MODEL
GLA_DK = GLA_KEY_WIDTH // GLA_HEADS
GLA_DV = GLA_VALUE_WIDTH // GLA_HEADS
GLA_GATE_RANK = 16
GLA_GATE_TAU = 16.0
D_FF = -(-8 * D_MODEL // (3 * 256)) * 256
RMS_EPS = 1e-6
LOG2E = 1.4426950408889634

LANES = 128
MXU_DIM = 256
VMEM_LIMIT_BYTES = 56 * 1024 * 1024

ROW_TILE = 512
INPROJ_ROW_TILE = 512
INPROJ_CHUNK = 512
INPROJ_AHEAD = 2
SB_TQ = 4096
SB_TK = 256
SB_QT = 128
SB_UNDERFLOW_LOG2 = -126.0
SB_NO_BLOCK_BIAS = -1e30
SB_LOGITS_AHEAD = 3
SB_WEIGHTS_BEHIND = 2
GLA_CHUNK = 64
GLA_VECTOR_LEVELS = (32, 16, 8)
GLA_MATMUL_LEVELS = (4, 2, 1)
GLA_LEVELS = GLA_VECTOR_LEVELS + GLA_MATMUL_LEVELS


def _const_spec(shape):
    nd = len(shape)
    return pl.BlockSpec(shape, lambda *_: (0,) * nd, pipeline_mode=pl.Buffered(1))


def _log_sigmoid(x):
    return jnp.minimum(x, 0.0) - jnp.log(1.0 + jnp.exp(-jnp.abs(x)))


def _inproj_gla_kernel(x_ref, g1_ref, w_ref, w2_ref, gb_ref, qg_ref, kg_ref, avg_ref,
                       cm_ref, ng_ref,
                       q_ref, k_ref, v_ref, gs_ref, gg_ref, og_ref,
                       wb_s, gq_s, gk_s, gv_s, sr_s, la_s, state_ref, *, steps_per_seq):
    tm = x_ref.shape[0]

    @pl.when(pl.program_id(0) % steps_per_seq == 0)
    def _():
        state_ref[...] = jnp.zeros_like(state_ref)

    x = x_ref[...]
    ms = jnp.mean(x * x, axis=-1, keepdims=True)
    h = (x * lax.rsqrt(ms + RMS_EPS) * g1_ref[...]).astype(BF16)
    rank0 = 3 * SB_WIDTH + 2 * GLA_KEY_WIDTH + 2 * GLA_VALUE_WIDTH
    rank1 = rank0 + GLA_GATE_RANK

    @pl.when(pl.program_id(0) == 0)
    def _():
        wb_s[...] = w_ref[:, rank1:]

    def proj(c0, cw):
        w = w_ref[:, c0:c0 + cw] if c0 < rank0 else wb_s[:, c0 - rank0:c0 - rank0 + cw]
        return jnp.dot(h, w, preferred_element_type=F32)

    def head_norm(y, gain):
        m = jnp.dot((y * y).astype(BF16), avg_ref[...], preferred_element_type=F32)
        return y * lax.rsqrt(m + RMS_EPS) * gain

    segments = (
        (q_ref, SB_WIDTH, MXU_DIM, lambda y: head_norm(y, qg_ref[...])),
        (k_ref, SB_WIDTH, MXU_DIM, lambda y: head_norm(y, kg_ref[...])),
        (v_ref, SB_WIDTH, INPROJ_CHUNK, lambda y: y),
        (gq_s, GLA_KEY_WIDTH, INPROJ_CHUNK, lambda y: y * (GLA_DK ** -0.5)),
        (gk_s, GLA_KEY_WIDTH, INPROJ_CHUNK, lambda y: y),
        (gv_s, GLA_VALUE_WIDTH, INPROJ_CHUNK, lambda y: y),
        (sr_s, GLA_VALUE_WIDTH, INPROJ_CHUNK, lambda y: y * jax.nn.sigmoid(y)),
        (gs_ref, D_MODEL, INPROJ_CHUNK, jax.nn.sigmoid),
        (gg_ref, D_MODEL, INPROJ_CHUNK, jax.nn.sigmoid),
    )
    feed, normed, plain, off = [], [], [], 0
    for ref, width, cw, post in segments:
        cw = min(cw, width)
        group = (normed if (ref is q_ref or ref is k_ref)
                 else feed if (ref is gq_s or ref is gk_s or ref is gv_s or ref is sr_s) else plain)
        group.extend((ref, lo, off + lo, cw, post) for lo in range(0, width, cw))
        off += width
    jobs = list(feed)
    while normed or plain:
        if plain:
            jobs.append(plain.pop(0))
        if normed:
            jobs.append(normed.pop(0))

    c = GLA_CHUNK
    cm = cm_ref[...]
    ri = lax.broadcasted_iota(jnp.int32, (c, c), 0)
    ci = lax.broadcasted_iota(jnp.int32, (c, c), 1)
    nt = (((1,), (1,)), ((), ()))
    tn = (((0,), (0,)), ((), ()))
    units = [(hh, slice(ch * c, (ch + 1) * c), slice(hh * GLA_DK, (hh + 1) * GLA_DK),
              slice(hh * GLA_DV, (hh + 1) * GLA_DV)) for hh in range(GLA_HEADS) for ch in range(tm // c)]
    n_units = len(units)
    level_masks = []
    for n in GLA_LEVELS:
        sh = n.bit_length() - 1
        level_masks.append(((ri >> sh) == (ci >> sh) + 1) & (((ri >> sh) & 1) == 1))

    args, o_intra, q_in = [None] * n_units, [None] * n_units, [None] * n_units
    decays, upds, states = [None] * n_units, [None] * n_units, [None] * GLA_HEADS

    def hi_lo(rows, kl):
        g = la_s[rows, kl]
        ghi = g.astype(BF16)
        glo = (g - ghi.astype(F32)).astype(BF16)
        return jnp.concatenate([ghi, glo], axis=0)

    def exponents(u):
        (_, r0, kl0, _), (_, r1, kl1, _) = units[u], units[u + 1]
        pair = jnp.dot(cm, jnp.concatenate([hi_lo(r0, kl0), hi_lo(r1, kl1)], axis=1),
                       preferred_element_type=F32)
        for i in range(2):
            part = pair[:, i * GLA_DK:(i + 1) * GLA_DK]
            cum = part[0:c]
            blocks = [cum, cum[c - 1:c, :] - cum]
            for n in GLA_VECTOR_LEVELS:
                for b0 in range(0, c, 2 * n):
                    pivot = cum[b0 + n - 1:b0 + n, :]
                    blocks += [pivot - cum[b0:b0 + n, :], cum[b0 + n:b0 + 2 * n, :] - pivot]
            args[u + i] = jnp.concatenate(blocks + [part[c:]], axis=0)

    def within_chunk(u):
        _, rows, kl, vl = units[u]
        arg = args[u]
        ex = jnp.exp(arg.astype(BF16))
        qb = gq_s[rows, kl]
        kb = gk_s[rows, kl]
        vb = gv_s[rows, vl]
        sc = jnp.where(ri == ci, lax.dot_general(qb, kb, nt, preferred_element_type=F32), 0.0)
        for li in range(len(GLA_LEVELS)):
            ex_l = ex[(li + 2) * c:(li + 3) * c]
            s_l = lax.dot_general(qb * ex_l, kb * ex_l, nt, preferred_element_type=F32)
            sc = jnp.where(level_masks[li], s_l, sc)
        o_intra[u] = jnp.dot(sc.astype(BF16), vb, preferred_element_type=F32)
        q_in[u] = qb * ex[0:c]
        upds[u] = lax.dot_general(kb * ex[c:2 * c], vb, tn, preferred_element_type=F32)
        last = arg[c - 1:c, :]
        decay = jnp.broadcast_to(jnp.exp(last), (GLA_DK, GLA_DK)).T
        decays[u] = jnp.concatenate([decay, decay], axis=1)

    def across_chunks(u):
        hh, rows, _, vl = units[u]
        if states[hh] is None:
            states[hh] = state_ref[hh]
        state = states[hh]
        o = o_intra[u] + jnp.dot(q_in[u], state.astype(BF16), preferred_element_type=F32)
        states[hh] = state * decays[u] + upds[u]
        ms_o = jnp.mean(o * o, axis=-1, keepdims=True)
        y = o * lax.rsqrt(ms_o + RMS_EPS) * ng_ref[...]
        og_ref[rows, vl] = (y * sr_s[rows, vl].astype(F32)).astype(BF16)
        if u + 1 == n_units or units[u + 1][0] != hh:
            state_ref[hh] = states[hh]

    pieces = ([functools.partial(exponents, u) for u in range(0, n_units, 2)]
              + [functools.partial(within_chunk, u) for u in range(n_units)]
              + [functools.partial(across_chunks, u) for u in range(n_units)])
    n_rest = len(jobs) - len(feed)
    per_job = -(-len(pieces) // n_rest)

    lr = jnp.dot(h, w_ref[:, rank0:rank0 + LANES], preferred_element_type=F32)
    issued = [proj(j[2], j[3]) for j in jobs[:INPROJ_AHEAD]]
    for n, (ref, lo, _, cw, post) in enumerate(jobs):
        y = issued.pop(0)
        if n + INPROJ_AHEAD < len(jobs):
            issued.append(proj(jobs[n + INPROJ_AHEAD][2], jobs[n + INPROJ_AHEAD][3]))
        if n == 0:
            logit = jnp.dot(lr.astype(BF16), w2_ref[...], preferred_element_type=F32) + gb_ref[...]
            la_s[...] = _log_sigmoid(logit) * (1.0 / GLA_GATE_TAU)
        ref[:, lo:lo + cw] = post(y).astype(BF16)
        if n >= len(feed):
            for piece in pieces[:per_job]:
                piece()
            del pieces[:per_job]
    assert not pieces


def _in_projection_gla(x2, seq_len, g1, w, w2p, gb, qg, kg, avg, cm, ng):
    t = x2.shape[0]
    tm = INPROJ_ROW_TILE
    row = pl.BlockSpec((tm, D_MODEL), lambda i: (i, 0))
    bf = jax.ShapeDtypeStruct((t, D_MODEL), BF16)
    consts = (g1, w, w2p, gb, qg, kg, avg, cm, ng)
    return pl.pallas_call(
        functools.partial(_inproj_gla_kernel, steps_per_seq=seq_len // tm),
        grid=(t // tm,),
        in_specs=[row] + [_const_spec(c.shape) for c in consts],
        out_specs=(row,) * 6,
        out_shape=(bf,) * 6,
        scratch_shapes=[pltpu.VMEM((D_MODEL, 2 * D_MODEL), BF16),
                        pltpu.VMEM((tm, GLA_KEY_WIDTH), BF16), pltpu.VMEM((tm, GLA_KEY_WIDTH), BF16),
                        pltpu.VMEM((tm, GLA_VALUE_WIDTH), BF16), pltpu.VMEM((tm, GLA_VALUE_WIDTH), BF16),
                        pltpu.VMEM((tm, GLA_KEY_WIDTH), F32),
                        pltpu.VMEM((GLA_HEADS, GLA_DK, GLA_DV), F32)],
        compiler_params=pltpu.CompilerParams(dimension_semantics=("arbitrary",),
                                             vmem_limit_bytes=VMEM_LIMIT_BYTES),
        name="in_projection_gla",
    )(x2, *consts)


def _sb_kernel(q_ref, k_ref, v_ref, mt_ref, o_ref, acc_ref, carry_ref):
    qi = pl.program_id(2)
    tq, tk, qt = q_ref.shape[1], SB_TK, SB_QT
    sub = tq // tk
    nt = (((1,), (1,)), ((), ()))
    tn = (((0,), (0,)), ((), ()))
    mt = mt_ref[...]
    one = jnp.ones((), BF16)
    zero16 = jnp.zeros((), BF16)
    first = qi * sub

    def head_split(q):
        lane = lax.broadcasted_iota(jnp.int32, q.shape, 1)
        zero = jnp.zeros_like(q)
        return (jnp.where(lane < SB_HEAD_DIM, q, zero), jnp.where(lane >= SB_HEAD_DIM, q, zero))

    def key_block(ref, j):
        return ref[0, pl.ds(pl.multiple_of(j * tk, tk), tk), :]

    def softplus2(zt):
        l2 = jnp.log2((one + jnp.exp2(-jnp.abs(zt))).astype(F32)).astype(BF16)
        return jnp.maximum(zt, zero16) + l2

    def causal(rows, width):
        return (lax.broadcasted_iota(jnp.int32, (rows, width), 0)
                < lax.broadcasted_iota(jnp.int32, (rows, width), 1))

    top_mask = causal(qt, tk)
    br_mask = causal(qt, qt)
    dead = jnp.zeros((qt, qt), BF16)

    def with_dead_quadrant(top, br):
        return jnp.concatenate([top, jnp.concatenate([dead, br], axis=1)], axis=0)

    def diag_logits(g):
        kb = key_block(k_ref, first + g)
        qh = head_split(q_ref[0, g * tk:(g + 1) * tk, :])
        return [lax.dot_general(kb, qh[h], nt, preferred_element_type=F32).astype(BF16)
                for h in range(2)]

    def diag_log_not(zts):
        out = []
        for zt in zts:
            p_top = jnp.where(top_mask, softplus2(zt[:qt, :]), zero16)
            p_br = jnp.where(br_mask, softplus2(zt[qt:, qt:]), zero16)
            cs = jnp.dot(mt, with_dead_quadrant(p_top, p_br), preferred_element_type=F32)
            out.append((p_top, p_br, cs))
        return out

    def diag_weights(g, zts, pcs):
        vb = key_block(v_ref, first + g)
        out = []
        for h in range(2):
            p_top, p_br, cs = pcs[h]
            zt = zts[h]
            w_top = jnp.exp2((zt[:qt, :] - p_top) + cs[:qt, :].astype(BF16))
            w_br = jnp.exp2((zt[qt:, qt:] - p_br) + cs[qt:, qt:].astype(BF16))
            w = with_dead_quadrant(jnp.where(top_mask, w_top, zero16), jnp.where(br_mask, w_br, zero16))
            pv = lax.dot_general(vb, w, tn, preferred_element_type=F32)
            out.append((pv, cs[0:1, :] - p_top[0:1, :].astype(F32)))
        return out

    mt_half = mt[:qt, :qt]

    def half_block(ref, j):
        return ref[0, pl.ds(pl.multiple_of(j * qt, qt), qt), :]

    def tile_logits(j, qh):
        kb = half_block(k_ref, j)
        return lax.dot_general(kb, jnp.concatenate(qh, axis=0), nt,
                               preferred_element_type=F32).astype(BF16)

    def tile_log_not(zt):
        p = softplus2(zt)
        return p, jnp.dot(mt_half, p, preferred_element_type=F32)

    def tile_weights(j, zt, pc, rows):
        vb = half_block(v_ref, j)
        p, cs = pc
        w = jnp.exp2(((zt - p) + cs.astype(BF16)) + jnp.concatenate(rows, axis=1).astype(BF16))
        pv = lax.dot_general(vb, w, tn, preferred_element_type=F32)
        tot = cs[0:1, :] - p[0:1, :].astype(F32)
        return [(pv[:, h * qt:(h + 1) * qt], tot[:, h * qt:(h + 1) * qt]) for h in range(2)]

    no_prev_bias = jnp.where(first > 0, 0.0, SB_NO_BLOCK_BIAS).astype(F32)
    halves_per_block = tk // qt
    prev_block = [jnp.maximum((first + g) * halves_per_block - 1, 0) for g in range(sub)]
    prev_q = [None] * sub
    n_units = 2 * sub
    zs, pcs, diag_out = {}, {}, {}

    def issue_logits(n):
        if n < sub:
            zs[n] = diag_logits(n)
        else:
            g = n - sub
            prev_q[g] = head_split(q_ref[0, g * tk:g * tk + qt, :])
            zs[n] = tile_logits(prev_block[g], prev_q[g])

    def issue_weights(n):
        if n < sub:
            diag_out[n] = diag_weights(n, zs.pop(n), pcs.pop(n))
            return
        g = n - sub
        rows = [diag_out[g][h][1][:, :qt] for h in range(2)]
        if g == 0:
            rows = [r + no_prev_bias for r in rows]
        res = tile_weights(prev_block[g], zs.pop(n), pcs.pop(n), rows)
        for h in range(2):
            pv_d, tot_d = diag_out[g][h]
            pv_p, tot_p = res[h]
            acc_ref[h, 2 * g] = pv_d[:, :qt] + pv_p
            acc_ref[h, 2 * g + 1] = pv_d[:, qt:]
            carry_ref[h, 2 * g:2 * g + 1, :] = tot_d[:, :qt] + tot_p
            carry_ref[h, 2 * g + 1:2 * g + 2, :] = tot_d[:, qt:]

    for n in range(SB_LOGITS_AHEAD):
        issue_logits(n)
    for n in range(n_units + SB_WEIGHTS_BEHIND):
        if n + SB_LOGITS_AHEAD < n_units:
            issue_logits(n + SB_LOGITS_AHEAD)
        if n < n_units:
            pcs[n] = diag_log_not(zs[n]) if n < sub else tile_log_not(zs[n])
        if n >= SB_WEIGHTS_BEHIND:
            issue_weights(n - SB_WEIGHTS_BEHIND)

    n_tiles = tq // qt
    tile_ids = lax.broadcasted_iota(jnp.int32, (n_tiles, 1), 0)

    def first_unvisited(c):
        return (first + (c >> 1)) * halves_per_block - 2 + (c & 1)

    next_block = first_unvisited(tile_ids)

    def next_live_tile(after):
        top = jnp.max(jnp.maximum(carry_ref[0], carry_ref[1]), axis=1, keepdims=True)
        todo = (top >= SB_UNDERFLOW_LOG2) & (next_block >= 0) & (tile_ids > after)
        return jnp.min(jnp.where(todo, tile_ids, n_tiles))

    def walk_tile(c):
        qh = head_split(q_ref[0, pl.ds(pl.multiple_of(c * qt, qt), qt), :])

        def live(state):
            j, top = state
            return jnp.logical_and(j >= 0, top >= SB_UNDERFLOW_LOG2)

        def step(state):
            j, _ = state
            zts = tile_logits(j, qh)
            rows = [carry_ref[h, pl.ds(c, 1), :] for h in range(2)]
            res = tile_weights(j, zts, tile_log_not(zts), rows)
            new_rows = [rows[h] + res[h][1] for h in range(2)]
            for h in range(2):
                acc_ref[h, c] += res[h][0]
                carry_ref[h, pl.ds(c, 1), :] = new_rows[h]
            return j - 1, jnp.max(jnp.maximum(new_rows[0], new_rows[1]))

        lax.while_loop(live, step, (first_unvisited(c), jnp.float32(0.0)))
        return next_live_tile(c)

    lax.while_loop(lambda c: c < n_tiles, walk_tile, next_live_tile(-1))

    row = lax.broadcasted_iota(jnp.int32, (LANES, qt), 0)
    for c in range(tq // qt):
        ot = jnp.where(row < SB_HEAD_DIM, acc_ref[0, c], acc_ref[1, c])
        o_ref[0, c * qt:(c + 1) * qt, :] = ot.T.astype(BF16)


def _sb_attention(q, k, v, mt):
    b, s, _ = q.shape
    tq = min(SB_TQ, s)
    qspec = pl.BlockSpec((1, tq, LANES), lambda bi, hp, i: (bi, i, hp))
    kvspec = pl.BlockSpec((1, s, LANES), lambda bi, hp, i: (bi, 0, hp))
    return pl.pallas_call(
        _sb_kernel,
        grid=(b, SB_WIDTH // LANES, s // tq),
        in_specs=[qspec, kvspec, kvspec, _const_spec(mt.shape)],
        out_specs=qspec,
        out_shape=jax.ShapeDtypeStruct((b, s, SB_WIDTH), BF16),
        scratch_shapes=[pltpu.VMEM((2, tq // SB_QT, LANES, SB_QT), F32),
                        pltpu.VMEM((2, tq // SB_QT, SB_QT), F32)],
        compiler_params=pltpu.CompilerParams(
            dimension_semantics=("arbitrary", "arbitrary", "arbitrary"),
            vmem_limit_bytes=VMEM_LIMIT_BYTES),
        name="stick_breaking_attention",
    )(q, k, v, mt)


def _gla_pivot_rows(n):
    rows = np.arange(GLA_CHUNK)
    base = (rows // n) * n
    second = (rows // n) % 2 == 1
    return second, np.where(second, base - 1, base + n - 1)


def _gla_level_matrix():
    c = GLA_CHUNK
    r = np.arange(c)[:, None]
    m = np.arange(c)[None, :]
    mats = [m <= r]
    for n in GLA_MATMUL_LEVELS:
        _, idx = _gla_pivot_rows(n)
        lo = np.minimum(r[:, 0], idx)[:, None]
        hi = np.maximum(r[:, 0], idx)[:, None]
        mats.append((m > lo) & (m <= hi))
    one = np.concatenate(mats, axis=0).astype(np.float32)
    return np.concatenate([one, one], axis=1)


def _tail_kernel(x_ref, osb_ref, ogla_ref, gs_ref, gg_ref, wsb_ref, wgla_ref, wout_ref,
                 g2_ref, wg_ref, wu_ref, wd_ref, o_ref):
    tm = x_ref.shape[0]
    halves = [slice(0, tm // 2), slice(tm // 2, tm)]
    dot = functools.partial(jnp.dot, preferred_element_type=F32)
    ab = [(dot(osb_ref[sl, :], wsb_ref[...]), dot(ogla_ref[sl, :], wgla_ref[...])) for sl in halves]
    merged = [(gs_ref[sl, :].astype(F32) * a + gg_ref[sl, :].astype(F32) * b).astype(BF16)
              for sl, (a, b) in zip(halves, ab)]
    x1 = [x_ref[sl, :] + dot(m, wout_ref[...]) for sl, m in zip(halves, merged)]
    h2 = []
    for x1h in x1:
        ms = jnp.mean(x1h * x1h, axis=-1, keepdims=True)
        h2.append((x1h * lax.rsqrt(ms + RMS_EPS) * g2_ref[...]).astype(BF16))
    gu = [(dot(h, wg_ref[...]), dot(h, wu_ref[...])) for h in h2]
    hid = [(gate * jax.nn.sigmoid(gate) * up).astype(BF16) for gate, up in gu]
    for sl, x1h, hd in zip(halves, x1, hid):
        o_ref[sl, :] = x1h + dot(hd, wd_ref[...])


def _tail(x2, osb, ogla, gs, gg, wsb, wgla, wout, g2, wg, wu, wd):
    t = x2.shape[0]
    tm = ROW_TILE
    row = pl.BlockSpec((tm, D_MODEL), lambda i: (i, 0))
    consts = (wsb, wgla, wout, g2, wg, wu, wd)
    return pl.pallas_call(
        _tail_kernel,
        grid=(t // tm,),
        in_specs=[row] * 5 + [_const_spec(c.shape) for c in consts],
        out_specs=row,
        out_shape=jax.ShapeDtypeStruct((t, D_MODEL), F32),
        compiler_params=pltpu.CompilerParams(dimension_semantics=("arbitrary",),
                                             vmem_limit_bytes=VMEM_LIMIT_BYTES),
        name="merge_out_ffn",
    )(x2, osb, ogla, gs, gg, *consts)


def _layer(x, norm1_g, w_in, sb_q_norm_g, sb_k_norm_g, gla_gate_w2, gla_gate_b, gla_out_norm_g,
           w_branch_sb, w_branch_gla, w_out, norm2_g, w_ffn_gate, w_ffn_up, w_ffn_down):
    b, s, d = x.shape
    t = b * s
    x2 = x.reshape(t, d)

    w2p = jnp.pad(gla_gate_w2, ((0, LANES - GLA_GATE_RANK), (0, 0))).astype(BF16)

    heads_per_tile = MXU_DIM // SB_HEAD_DIM
    qg = jnp.tile(sb_q_norm_g * (SB_HEAD_DIM ** -0.5 * LOG2E), heads_per_tile).reshape(1, MXU_DIM)
    kg = jnp.tile(sb_k_norm_g, heads_per_tile).reshape(1, MXU_DIM)
    hid = np.arange(MXU_DIM) // SB_HEAD_DIM
    avg = jnp.asarray((hid[:, None] == hid[None, :]).astype(np.float32) / SB_HEAD_DIM, BF16)

    cm = jnp.asarray(_gla_level_matrix(), BF16)
    q, k, v, gs, gg, o_gla = _in_projection_gla(
        x2, s, norm1_g.reshape(1, d), w_in.astype(BF16), w2p, gla_gate_b.reshape(1, -1), qg, kg, avg,
        cm, gla_out_norm_g.reshape(1, -1))

    idx = np.arange(SB_TK)
    mt = jnp.asarray(-(idx[None, :] > idx[:, None]).astype(np.float32), BF16)
    o_sb = _sb_attention(q.reshape(b, s, -1), k.reshape(b, s, -1), v.reshape(b, s, -1), mt)

    out = _tail(x2, o_sb.reshape(t, -1), o_gla, gs, gg,
                w_branch_sb.astype(BF16), w_branch_gla.astype(BF16), w_out.astype(BF16),
                norm2_g.reshape(1, d), w_ffn_gate.astype(BF16), w_ffn_up.astype(BF16),
                w_ffn_down.astype(BF16))
    return out.reshape(b, s, d)


def kernel(x, norm1_g, w_in, sb_q_norm_g, sb_k_norm_g, gla_gate_w2, gla_gate_b, gla_out_norm_g,
           w_branch_sb, w_branch_gla, w_out, norm2_g, w_ffn_gate, w_ffn_up, w_ffn_down):
    for l in range(norm1_g.shape[0]):
        x = _layer(x, norm1_g[l], w_in[l], sb_q_norm_g[l], sb_k_norm_g[l], gla_gate_w2[l],
                   gla_gate_b[l], gla_out_norm_g[l], w_branch_sb[l], w_branch_gla[l], w_out[l],
                   norm2_g[l], w_ffn_gate[l], w_ffn_up[l], w_ffn_down[l])
    return x
```

```python
import functools

import jax
import jax.numpy as jnp
import numpy as np
from jax import lax
from jax.experimental import pallas as pl
from jax.experimental.pallas import tpu as pltpu

F32 = jnp.float32
BF16 = jnp.bfloat16

D_MODEL = 1024
SB_HEADS = 16
SB_HEAD_DIM = 64
SB_WIDTH = SB_HEADS * SB_HEAD_DIM
GLA_HEADS = 4
GLA_KEY_WIDTH = D_MODEL // 2
GLA_VALUE_WIDTH = D_MODEL
GLA_DK = GLA_KEY_WIDTH // GLA_HEADS
GLA_DV = GLA_VALUE_WIDTH // GLA_HEADS
GLA_GATE_RANK = 16
GLA_GATE_TAU = 16.0
D_FF = -(-8 * D_MODEL // (3 * 256)) * 256
RMS_EPS = 1e-6
LOG2E = 1.4426950408889634

LANES = 128
MXU_DIM = 256
VMEM_LIMIT_BYTES = 56 * 1024 * 1024

ROW_TILE = 512
INPROJ_ROW_TILE = 512
INPROJ_CHUNK = 512
INPROJ_AHEAD = 2
SB_TQ = 4096
SB_TK = 256
SB_QT = 128
SB_UNDERFLOW_LOG2 = -126.0
SB_NO_BLOCK_BIAS = -1e30
SB_LOGITS_AHEAD = 3
SB_WEIGHTS_BEHIND = 2
GLA_CHUNK = 64
GLA_VECTOR_LEVELS = (32, 16, 8)
GLA_MATMUL_LEVELS = (4, 2, 1)
GLA_LEVELS = GLA_VECTOR_LEVELS + GLA_MATMUL_LEVELS


def _const_spec(shape):
    nd = len(shape)
    return pl.BlockSpec(shape, lambda *_: (0,) * nd, pipeline_mode=pl.Buffered(1))


def _log_sigmoid(x):
    return jnp.minimum(x, 0.0) - jnp.log(1.0 + jnp.exp(-jnp.abs(x)))


def _inproj_gla_kernel(x_ref, g1_ref, w_ref, w2_ref, gb_ref, qg_ref, kg_ref, avg_ref,
                       cm_ref, ng_ref,
                       q_ref, k_ref, v_ref, gs_ref, gg_ref, og_ref,
                       wb_s, gq_s, gk_s, gv_s, sr_s, la_s, state_ref, *, steps_per_seq):
    tm = x_ref.shape[0]

    @pl.when(pl.program_id(0) % steps_per_seq == 0)
    def _():
        state_ref[...] = jnp.zeros_like(state_ref)

    x = x_ref[...]
    ms = jnp.mean(x * x, axis=-1, keepdims=True)
    h = (x * lax.rsqrt(ms + RMS_EPS) * g1_ref[...]).astype(BF16)
    rank0 = 3 * SB_WIDTH + 2 * GLA_KEY_WIDTH + 2 * GLA_VALUE_WIDTH
    rank1 = rank0 + GLA_GATE_RANK

    @pl.when(pl.program_id(0) == 0)
    def _():
        wb_s[...] = w_ref[:, rank1:]

    def proj(c0, cw):
        w = w_ref[:, c0:c0 + cw] if c0 < rank0 else wb_s[:, c0 - rank0:c0 - rank0 + cw]
        return jnp.dot(h, w, preferred_element_type=F32)

    def head_norm(y, gain):
        m = jnp.dot((y * y).astype(BF16), avg_ref[...], preferred_element_type=F32)
        return y * lax.rsqrt(m + RMS_EPS) * gain

    segments = (
        (q_ref, SB_WIDTH, MXU_DIM, lambda y: head_norm(y, qg_ref[...])),
        (k_ref, SB_WIDTH, MXU_DIM, lambda y: head_norm(y, kg_ref[...])),
        (v_ref, SB_WIDTH, INPROJ_CHUNK, lambda y: y),
        (gq_s, GLA_KEY_WIDTH, INPROJ_CHUNK, lambda y: y * (GLA_DK ** -0.5)),
        (gk_s, GLA_KEY_WIDTH, INPROJ_CHUNK, lambda y: y),
        (gv_s, GLA_VALUE_WIDTH, INPROJ_CHUNK, lambda y: y),
        (sr_s, GLA_VALUE_WIDTH, INPROJ_CHUNK, lambda y: y * jax.nn.sigmoid(y)),
        (gs_ref, D_MODEL, INPROJ_CHUNK, jax.nn.sigmoid),
        (gg_ref, D_MODEL, INPROJ_CHUNK, jax.nn.sigmoid),
    )
    feed, normed, plain, off = [], [], [], 0
    for ref, width, cw, post in segments:
        cw = min(cw, width)
        group = (normed if (ref is q_ref or ref is k_ref)
                 else feed if (ref is gq_s or ref is gk_s or ref is gv_s or ref is sr_s) else plain)
        group.extend((ref, lo, off + lo, cw, post) for lo in range(0, width, cw))
        off += width
    jobs = list(feed)
    while normed or plain:
        if plain:
            jobs.append(plain.pop(0))
        if normed:
            jobs.append(normed.pop(0))

    c = GLA_CHUNK
    cm = cm_ref[...]
    ri = lax.broadcasted_iota(jnp.int32, (c, c), 0)
    ci = lax.broadcasted_iota(jnp.int32, (c, c), 1)
    nt = (((1,), (1,)), ((), ()))
    tn = (((0,), (0,)), ((), ()))
    units = [(hh, slice(ch * c, (ch + 1) * c), slice(hh * GLA_DK, (hh + 1) * GLA_DK),
              slice(hh * GLA_DV, (hh + 1) * GLA_DV)) for hh in range(GLA_HEADS) for ch in range(tm // c)]
    n_units = len(units)
    level_masks = []
    for n in GLA_LEVELS:
        sh = n.bit_length() - 1
        level_masks.append(((ri >> sh) == (ci >> sh) + 1) & (((ri >> sh) & 1) == 1))

    args, o_intra, q_in = [None] * n_units, [None] * n_units, [None] * n_units
    decays, upds, states = [None] * n_units, [None] * n_units, [None] * GLA_HEADS

    def hi_lo(rows, kl):
        g = la_s[rows, kl]
        ghi = g.astype(BF16)
        glo = (g - ghi.astype(F32)).astype(BF16)
        return jnp.concatenate([ghi, glo], axis=0)

    def exponents(u):
        (_, r0, kl0, _), (_, r1, kl1, _) = units[u], units[u + 1]
        pair = jnp.dot(cm, jnp.concatenate([hi_lo(r0, kl0), hi_lo(r1, kl1)], axis=1),
                       preferred_element_type=F32)
        for i in range(2):
            part = pair[:, i * GLA_DK:(i + 1) * GLA_DK]
            cum = part[0:c]
            blocks = [cum, cum[c - 1:c, :] - cum]
            for n in GLA_VECTOR_LEVELS:
                for b0 in range(0, c, 2 * n):
                    pivot = cum[b0 + n - 1:b0 + n, :]
                    blocks += [pivot - cum[b0:b0 + n, :], cum[b0 + n:b0 + 2 * n, :] - pivot]
            args[u + i] = jnp.concatenate(blocks + [part[c:]], axis=0)

    def within_chunk(u):
        _, rows, kl, vl = units[u]
        arg = args[u]
        ex = jnp.exp(arg.astype(BF16))
        qb = gq_s[rows, kl]
        kb = gk_s[rows, kl]
        vb = gv_s[rows, vl]
        sc = jnp.where(ri == ci, lax.dot_general(qb, kb, nt, preferred_element_type=F32), 0.0)
        for li in range(len(GLA_LEVELS)):
            ex_l = ex[(li + 2) * c:(li + 3) * c]
            s_l = lax.dot_general(qb * ex_l, kb * ex_l, nt, preferred_element_type=F32)
            sc = jnp.where(level_masks[li], s_l, sc)
        o_intra[u] = jnp.dot(sc.astype(BF16), vb, preferred_element_type=F32)
        q_in[u] = qb * ex[0:c]
        upds[u] = lax.dot_general(kb * ex[c:2 * c], vb, tn, preferred_element_type=F32)
        last = arg[c - 1:c, :]
        decay = jnp.broadcast_to(jnp.exp(last), (GLA_DK, GLA_DK)).T
        decays[u] = jnp.concatenate([decay, decay], axis=1)

    def across_chunks(u):
        hh, rows, _, vl = units[u]
        if states[hh] is None:
            states[hh] = state_ref[hh]
        state = states[hh]
        o = o_intra[u] + jnp.dot(q_in[u], state.astype(BF16), preferred_element_type=F32)
        states[hh] = state * decays[u] + upds[u]
        ms_o = jnp.mean(o * o, axis=-1, keepdims=True)
        y = o * lax.rsqrt(ms_o + RMS_EPS) * ng_ref[...]
        og_ref[rows, vl] = (y * sr_s[rows, vl].astype(F32)).astype(BF16)
        if u + 1 == n_units or units[u + 1][0] != hh:
            state_ref[hh] = states[hh]

    pieces = ([functools.partial(exponents, u) for u in range(0, n_units, 2)]
              + [functools.partial(within_chunk, u) for u in range(n_units)]
              + [functools.partial(across_chunks, u) for u in range(n_units)])
    n_rest = len(jobs) - len(feed)
    per_job = -(-len(pieces) // n_rest)

    lr = jnp.dot(h, w_ref[:, rank0:rank0 + LANES], preferred_element_type=F32)
    issued = [proj(j[2], j[3]) for j in jobs[:INPROJ_AHEAD]]
    for n, (ref, lo, _, cw, post) in enumerate(jobs):
        y = issued.pop(0)
        if n + INPROJ_AHEAD < len(jobs):
            issued.append(proj(jobs[n + INPROJ_AHEAD][2], jobs[n + INPROJ_AHEAD][3]))
        if n == 0:
            logit = jnp.dot(lr.astype(BF16), w2_ref[...], preferred_element_type=F32) + gb_ref[...]
            la_s[...] = _log_sigmoid(logit) * (1.0 / GLA_GATE_TAU)
        ref[:, lo:lo + cw] = post(y).astype(BF16)
        if n >= len(feed):
            for piece in pieces[:per_job]:
                piece()
            del pieces[:per_job]
    assert not pieces


def _in_projection_gla(x2, seq_len, g1, w, w2p, gb, qg, kg, avg, cm, ng):
    t = x2.shape[0]
    tm = INPROJ_ROW_TILE
    row = pl.BlockSpec((tm, D_MODEL), lambda i: (i, 0))
    bf = jax.ShapeDtypeStruct((t, D_MODEL), BF16)
    consts = (g1, w, w2p, gb, qg, kg, avg, cm, ng)
    return pl.pallas_call(
        functools.partial(_inproj_gla_kernel, steps_per_seq=seq_len // tm),
        grid=(t // tm,),
        in_specs=[row] + [_const_spec(c.shape) for c in consts],
        out_specs=(row,) * 6,
        out_shape=(bf,) * 6,
        scratch_shapes=[pltpu.VMEM((D_MODEL, 2 * D_MODEL), BF16),
                        pltpu.VMEM((tm, GLA_KEY_WIDTH), BF16), pltpu.VMEM((tm, GLA_KEY_WIDTH), BF16),
                        pltpu.VMEM((tm, GLA_VALUE_WIDTH), BF16), pltpu.VMEM((tm, GLA_VALUE_WIDTH), BF16),
                        pltpu.VMEM((tm, GLA_KEY_WIDTH), F32),
                        pltpu.VMEM((GLA_HEADS, GLA_DK, GLA_DV), F32)],
        compiler_params=pltpu.CompilerParams(dimension_semantics=("arbitrary",),
                                             vmem_limit_bytes=VMEM_LIMIT_BYTES),
        name="in_projection_gla",
    )(x2, *consts)


def _sb_kernel(q_ref, k_ref, v_ref, mt_ref, o_ref, acc_ref, carry_ref):
    qi = pl.program_id(2)
    tq, tk, qt = q_ref.shape[1], SB_TK, SB_QT
    sub = tq // tk
    nt = (((1,), (1,)), ((), ()))
    tn = (((0,), (0,)), ((), ()))
    mt = mt_ref[...]
    one = jnp.ones((), BF16)
    zero16 = jnp.zeros((), BF16)
    first = qi * sub

    def head_split(q):
        lane = lax.broadcasted_iota(jnp.int32, q.shape, 1)
        zero = jnp.zeros_like(q)
        return (jnp.where(lane < SB_HEAD_DIM, q, zero), jnp.where(lane >= SB_HEAD_DIM, q, zero))

    def key_block(ref, j):
        return ref[0, pl.ds(pl.multiple_of(j * tk, tk), tk), :]

    def softplus2(zt):
        l2 = jnp.log2((one + jnp.exp2(-jnp.abs(zt))).astype(F32)).astype(BF16)
        return jnp.maximum(zt, zero16) + l2

    def causal(rows, width):
        return (lax.broadcasted_iota(jnp.int32, (rows, width), 0)
                < lax.broadcasted_iota(jnp.int32, (rows, width), 1))

    top_mask = causal(qt, tk)
    br_mask = causal(qt, qt)
    dead = jnp.zeros((qt, qt), BF16)

    def with_dead_quadrant(top, br):
        return jnp.concatenate([top, jnp.concatenate([dead, br], axis=1)], axis=0)

    def diag_logits(g):
        kb = key_block(k_ref, first + g)
        qh = head_split(q_ref[0, g * tk:(g + 1) * tk, :])
        return [lax.dot_general(kb, qh[h], nt, preferred_element_type=F32).astype(BF16)
                for h in range(2)]

    def diag_log_not(zts):
        out = []
        for zt in zts:
            p_top = jnp.where(top_mask, softplus2(zt[:qt, :]), zero16)
            p_br = jnp.where(br_mask, softplus2(zt[qt:, qt:]), zero16)
            cs = jnp.dot(mt, with_dead_quadrant(p_top, p_br), preferred_element_type=F32)
            out.append((p_top, p_br, cs))
        return out

    def diag_weights(g, zts, pcs):
        vb = key_block(v_ref, first + g)
        out = []
        for h in range(2):
            p_top, p_br, cs = pcs[h]
            zt = zts[h]
            w_top = jnp.exp2((zt[:qt, :] - p_top) + cs[:qt, :].astype(BF16))
            w_br = jnp.exp2((zt[qt:, qt:] - p_br) + cs[qt:, qt:].astype(BF16))
            w = with_dead_quadrant(jnp.where(top_mask, w_top, zero16), jnp.where(br_mask, w_br, zero16))
            pv = lax.dot_general(vb, w, tn, preferred_element_type=F32)
            out.append((pv, cs[0:1, :] - p_top[0:1, :].astype(F32)))
        return out

    mt_half = mt[:qt, :qt]

    def half_block(ref, j):
        return ref[0, pl.ds(pl.multiple_of(j * qt, qt), qt), :]

    def tile_logits(j, qh):
        kb = half_block(k_ref, j)
        return lax.dot_general(kb, jnp.concatenate(qh, axis=0), nt,
                               preferred_element_type=F32).astype(BF16)

    def tile_log_not(zt):
        p = softplus2(zt)
        return p, jnp.dot(mt_half, p, preferred_element_type=F32)

    def tile_weights(j, zt, pc, rows):
        vb = half_block(v_ref, j)
        p, cs = pc
        w = jnp.exp2(((zt - p) + cs.astype(BF16)) + jnp.concatenate(rows, axis=1).astype(BF16))
        pv = lax.dot_general(vb, w, tn, preferred_element_type=F32)
        tot = cs[0:1, :] - p[0:1, :].astype(F32)
        return [(pv[:, h * qt:(h + 1) * qt], tot[:, h * qt:(h + 1) * qt]) for h in range(2)]

    no_prev_bias = jnp.where(first > 0, 0.0, SB_NO_BLOCK_BIAS).astype(F32)
    halves_per_block = tk // qt
    prev_block = [jnp.maximum((first + g) * halves_per_block - 1, 0) for g in range(sub)]
    prev_q = [None] * sub
    n_units = 2 * sub
    zs, pcs, diag_out = {}, {}, {}

    def issue_logits(n):
        if n < sub:
            zs[n] = diag_logits(n)
        else:
            g = n - sub
            prev_q[g] = head_split(q_ref[0, g * tk:g * tk + qt, :])
            zs[n] = tile_logits(prev_block[g], prev_q[g])

    def issue_weights(n):
        if n < sub:
            diag_out[n] = diag_weights(n, zs.pop(n), pcs.pop(n))
            return
        g = n - sub
        rows = [diag_out[g][h][1][:, :qt] for h in range(2)]
        if g == 0:
            rows = [r + no_prev_bias for r in rows]
        res = tile_weights(prev_block[g], zs.pop(n), pcs.pop(n), rows)
        for h in range(2):
            pv_d, tot_d = diag_out[g][h]
            pv_p, tot_p = res[h]
            acc_ref[h, 2 * g] = pv_d[:, :qt] + pv_p
            acc_ref[h, 2 * g + 1] = pv_d[:, qt:]
            carry_ref[h, 2 * g:2 * g + 1, :] = tot_d[:, :qt] + tot_p
            carry_ref[h, 2 * g + 1:2 * g + 2, :] = tot_d[:, qt:]

    for n in range(SB_LOGITS_AHEAD):
        issue_logits(n)
    for n in range(n_units + SB_WEIGHTS_BEHIND):
        if n + SB_LOGITS_AHEAD < n_units:
            issue_logits(n + SB_LOGITS_AHEAD)
        if n < n_units:
            pcs[n] = diag_log_not(zs[n]) if n < sub else tile_log_not(zs[n])
        if n >= SB_WEIGHTS_BEHIND:
            issue_weights(n - SB_WEIGHTS_BEHIND)

    n_tiles = tq // qt
    tile_ids = lax.broadcasted_iota(jnp.int32, (n_tiles, 1), 0)

    def first_unvisited(c):
        return (first + (c >> 1)) * halves_per_block - 2 + (c & 1)

    next_block = first_unvisited(tile_ids)

    top = jnp.max(jnp.maximum(carry_ref[0], carry_ref[1]), axis=1, keepdims=True)
    todo = jnp.where((top >= SB_UNDERFLOW_LOG2) & (next_block >= 0), tile_ids, n_tiles)

    def next_live_tile(after):
        return jnp.min(jnp.where(todo > after, todo, n_tiles))

    def walk_tile(c):
        qh = head_split(q_ref[0, pl.ds(pl.multiple_of(c * qt, qt), qt), :])

        def live(state):
            j, top = state
            return jnp.logical_and(j >= 0, top >= SB_UNDERFLOW_LOG2)

        def step(state):
            j, _ = state
            zts = tile_logits(j, qh)
            rows = [carry_ref[h, pl.ds(c, 1), :] for h in range(2)]
            res = tile_weights(j, zts, tile_log_not(zts), rows)
            new_rows = [rows[h] + res[h][1] for h in range(2)]
            for h in range(2):
                acc_ref[h, c] += res[h][0]
                carry_ref[h, pl.ds(c, 1), :] = new_rows[h]
            return j - 1, jnp.max(jnp.maximum(new_rows[0], new_rows[1]))

        lax.while_loop(live, step, (first_unvisited(c), jnp.float32(0.0)))
        return next_live_tile(c)

    lax.while_loop(lambda c: c < n_tiles, walk_tile, next_live_tile(-1))

    row = lax.broadcasted_iota(jnp.int32, (LANES, qt), 0)
    for c in range(tq // qt):
        ot = jnp.where(row < SB_HEAD_DIM, acc_ref[0, c], acc_ref[1, c])
        o_ref[0, c * qt:(c + 1) * qt, :] = ot.T.astype(BF16)


def _sb_attention(q, k, v, mt):
    b, s, _ = q.shape
    tq = min(SB_TQ, s)
    qspec = pl.BlockSpec((1, tq, LANES), lambda bi, hp, i: (bi, i, hp))
    kvspec = pl.BlockSpec((1, s, LANES), lambda bi, hp, i: (bi, 0, hp))
    return pl.pallas_call(
        _sb_kernel,
        grid=(b, SB_WIDTH // LANES, s // tq),
        in_specs=[qspec, kvspec, kvspec, _const_spec(mt.shape)],
        out_specs=qspec,
        out_shape=jax.ShapeDtypeStruct((b, s, SB_WIDTH), BF16),
        scratch_shapes=[pltpu.VMEM((2, tq // SB_QT, LANES, SB_QT), F32),
                        pltpu.VMEM((2, tq // SB_QT, SB_QT), F32)],
        compiler_params=pltpu.CompilerParams(
            dimension_semantics=("arbitrary", "arbitrary", "arbitrary"),
            vmem_limit_bytes=VMEM_LIMIT_BYTES),
        name="stick_breaking_attention",
    )(q, k, v, mt)


def _gla_pivot_rows(n):
    rows = np.arange(GLA_CHUNK)
    base = (rows // n) * n
    second = (rows // n) % 2 == 1
    return second, np.where(second, base - 1, base + n - 1)


def _gla_level_matrix():
    c = GLA_CHUNK
    r = np.arange(c)[:, None]
    m = np.arange(c)[None, :]
    mats = [m <= r]
    for n in GLA_MATMUL_LEVELS:
        _, idx = _gla_pivot_rows(n)
        lo = np.minimum(r[:, 0], idx)[:, None]
        hi = np.maximum(r[:, 0], idx)[:, None]
        mats.append((m > lo) & (m <= hi))
    one = np.concatenate(mats, axis=0).astype(np.float32)
    return np.concatenate([one, one], axis=1)


def _tail_kernel(x_ref, osb_ref, ogla_ref, gs_ref, gg_ref, wsb_ref, wgla_ref, wout_ref,
                 g2_ref, wg_ref, wu_ref, wd_ref, o_ref):
    tm = x_ref.shape[0]
    halves = [slice(0, tm // 2), slice(tm // 2, tm)]
    dot = functools.partial(jnp.dot, preferred_element_type=F32)
    ab = [(dot(osb_ref[sl, :], wsb_ref[...]), dot(ogla_ref[sl, :], wgla_ref[...])) for sl in halves]
    merged = [(gs_ref[sl, :].astype(F32) * a + gg_ref[sl, :].astype(F32) * b).astype(BF16)
              for sl, (a, b) in zip(halves, ab)]
    x1 = [x_ref[sl, :] + dot(m, wout_ref[...]) for sl, m in zip(halves, merged)]
    h2 = []
    for x1h in x1:
        ms = jnp.mean(x1h * x1h, axis=-1, keepdims=True)
        h2.append((x1h * lax.rsqrt(ms + RMS_EPS) * g2_ref[...]).astype(BF16))
    gu = [(dot(h, wg_ref[...]), dot(h, wu_ref[...])) for h in h2]
    hid = [(gate * jax.nn.sigmoid(gate) * up).astype(BF16) for gate, up in gu]
    for sl, x1h, hd in zip(halves, x1, hid):
        o_ref[sl, :] = x1h + dot(hd, wd_ref[...])


def _tail(x2, osb, ogla, gs, gg, wsb, wgla, wout, g2, wg, wu, wd):
    t = x2.shape[0]
    tm = ROW_TILE
    row = pl.BlockSpec((tm, D_MODEL), lambda i: (i, 0))
    consts = (wsb, wgla, wout, g2, wg, wu, wd)
    return pl.pallas_call(
        _tail_kernel,
        grid=(t // tm,),
        in_specs=[row] * 5 + [_const_spec(c.shape) for c in consts],
        out_specs=row,
        out_shape=jax.ShapeDtypeStruct((t, D_MODEL), F32),
        compiler_params=pltpu.CompilerParams(dimension_semantics=("arbitrary",),
                                             vmem_limit_bytes=VMEM_LIMIT_BYTES),
        name="merge_out_ffn",
    )(x2, osb, ogla, gs, gg, *consts)


def _layer(x, norm1_g, w_in, sb_q_norm_g, sb_k_norm_g, gla_gate_w2, gla_gate_b, gla_out_norm_g,
           w_branch_sb, w_branch_gla, w_out, norm2_g, w_ffn_gate, w_ffn_up, w_ffn_down):
    b, s, d = x.shape
    t = b * s
    x2 = x.reshape(t, d)

    w2p = jnp.pad(gla_gate_w2, ((0, LANES - GLA_GATE_RANK), (0, 0))).astype(BF16)

    heads_per_tile = MXU_DIM // SB_HEAD_DIM
    qg = jnp.tile(sb_q_norm_g * (SB_HEAD_DIM ** -0.5 * LOG2E), heads_per_tile).reshape(1, MXU_DIM)
    kg = jnp.tile(sb_k_norm_g, heads_per_tile).reshape(1, MXU_DIM)
    hid = np.arange(MXU_DIM) // SB_HEAD_DIM
    avg = jnp.asarray((hid[:, None] == hid[None, :]).astype(np.float32) / SB_HEAD_DIM, BF16)

    cm = jnp.asarray(_gla_level_matrix(), BF16)
    q, k, v, gs, gg, o_gla = _in_projection_gla(
        x2, s, norm1_g.reshape(1, d), w_in.astype(BF16), w2p, gla_gate_b.reshape(1, -1), qg, kg, avg,
        cm, gla_out_norm_g.reshape(1, -1))

    idx = np.arange(SB_TK)
    mt = jnp.asarray(-(idx[None, :] > idx[:, None]).astype(np.float32), BF16)
    o_sb = _sb_attention(q.reshape(b, s, -1), k.reshape(b, s, -1), v.reshape(b, s, -1), mt)

    out = _tail(x2, o_sb.reshape(t, -1), o_gla, gs, gg,
                w_branch_sb.astype(BF16), w_branch_gla.astype(BF16), w_out.astype(BF16),
                norm2_g.reshape(1, d), w_ffn_gate.astype(BF16), w_ffn_up.astype(BF16),
                w_ffn_down.astype(BF16))
    return out.reshape(b, s, d)


def kernel(x, norm1_g, w_in, sb_q_norm_g, sb_k_norm_g, gla_gate_w2, gla_gate_b, gla_out_norm_g,
           w_branch_sb, w_branch_gla, w_out, norm2_g, w_ffn_gate, w_ffn_up, w_ffn_down):
    for l in range(norm1_g.shape[0]):
        x = _layer(x, norm1_g[l], w_in[l], sb_q_norm_g[l], sb_k_norm_g[l], gla_gate_w2[l],
                   gla_gate_b[l], gla_out_norm_g[l], w_branch_sb[l], w_branch_gla[l], w_out[l],
                   norm2_g[l], w_ffn_gate[l], w_ffn_up[l], w_ffn_down[l])
    return x
```

```python
import functools

import jax
import jax.numpy as jnp
import numpy as np
from jax import lax
from jax.experimental import pallas as pl
from jax.experimental.pallas import tpu as pltpu

F32 = jnp.float32
BF16 = jnp.bfloat16

D_MODEL = 1024
SB_HEADS = 16
SB_HEAD_DIM = 64
SB_WIDTH = SB_HEADS * SB_HEAD_DIM
GLA_HEADS = 4
GLA_KEY_WIDTH = D_MODEL // 2
GLA_VALUE_WIDTH = D_MODEL
GLA_DK = GLA_KEY_WIDTH // GLA_HEADS
GLA_DV = GLA_VALUE_WIDTH // GLA_HEADS
GLA_GATE_RANK = 16
GLA_GATE_TAU = 16.0
RMS_EPS = 1e-6
LOG2E = 1.4426950408889634

LANES = 128
MXU_DIM = 256
VMEM_LIMIT_BYTES = 56 * 1024 * 1024

ROW_TILE = 512
INPROJ_ROW_TILE = 512
INPROJ_CHUNK = 512
INPROJ_AHEAD = 2
SB_TQ = 4096
SB_TK = 256
SB_QT = 128
SB_UNDERFLOW_LOG2 = -126.0
SB_NO_BLOCK_BIAS = -1e30
SB_LOGITS_AHEAD = 3
SB_WEIGHTS_BEHIND = 2
GLA_CHUNK = 64
GLA_VECTOR_LEVELS = (32, 16, 8)
GLA_MATMUL_LEVELS = (4, 2, 1)
GLA_LEVELS = GLA_VECTOR_LEVELS + GLA_MATMUL_LEVELS
assert GLA_DV == 2 * GLA_DK


def _const_spec(shape):
    nd = len(shape)
    return pl.BlockSpec(shape, lambda *_: (0,) * nd, pipeline_mode=pl.Buffered(1))


def _log_sigmoid(x):
    return jnp.minimum(x, 0.0) - jnp.log(1.0 + jnp.exp(-jnp.abs(x)))


def _inproj_gla_kernel(x_ref, g1_ref, w_ref, w2_ref, gb_ref, qg_ref, kg_ref, avg_ref,
                       cm_ref, ng_ref,
                       q_ref, k_ref, v_ref, gs_ref, gg_ref, og_ref,
                       wb_s, gq_s, gk_s, gv_s, sr_s, la_s, state_ref, *, steps_per_seq):
    tm = x_ref.shape[0]

    @pl.when(pl.program_id(0) % steps_per_seq == 0)
    def _():
        state_ref[...] = jnp.zeros_like(state_ref)

    x = x_ref[...]
    ms = jnp.mean(x * x, axis=-1, keepdims=True)
    h = (x * lax.rsqrt(ms + RMS_EPS) * g1_ref[...]).astype(BF16)
    rank0 = 3 * SB_WIDTH + 2 * GLA_KEY_WIDTH + 2 * GLA_VALUE_WIDTH
    rank1 = rank0 + GLA_GATE_RANK

    @pl.when(pl.program_id(0) == 0)
    def _():
        wb_s[...] = w_ref[:, rank1:]

    def proj(c0, cw):
        w = w_ref[:, c0:c0 + cw] if c0 < rank0 else wb_s[:, c0 - rank0:c0 - rank0 + cw]
        return jnp.dot(h, w, preferred_element_type=F32)

    def head_norm(y, gain):
        m = jnp.dot((y * y).astype(BF16), avg_ref[...], preferred_element_type=F32)
        return y * lax.rsqrt(m + RMS_EPS) * gain

    segments = (
        (q_ref, SB_WIDTH, MXU_DIM, lambda y: head_norm(y, qg_ref[...])),
        (k_ref, SB_WIDTH, MXU_DIM, lambda y: head_norm(y, kg_ref[...])),
        (v_ref, SB_WIDTH, INPROJ_CHUNK, lambda y: y),
        (gq_s, GLA_KEY_WIDTH, INPROJ_CHUNK, lambda y: y * (GLA_DK ** -0.5)),
        (gk_s, GLA_KEY_WIDTH, INPROJ_CHUNK, lambda y: y),
        (gv_s, GLA_VALUE_WIDTH, INPROJ_CHUNK, lambda y: y),
        (sr_s, GLA_VALUE_WIDTH, INPROJ_CHUNK, lambda y: y * jax.nn.sigmoid(y)),
        (gs_ref, D_MODEL, INPROJ_CHUNK, jax.nn.sigmoid),
        (gg_ref, D_MODEL, INPROJ_CHUNK, jax.nn.sigmoid),
    )
    feed, normed, plain, off = [], [], [], 0
    for ref, width, cw, post in segments:
        cw = min(cw, width)
        group = (normed if (ref is q_ref or ref is k_ref)
                 else feed if (ref is gq_s or ref is gk_s or ref is gv_s or ref is sr_s) else plain)
        group.extend((ref, lo, off + lo, cw, post) for lo in range(0, width, cw))
        off += width
    jobs = list(feed)
    while normed or plain:
        if plain:
            jobs.append(plain.pop(0))
        if normed:
            jobs.append(normed.pop(0))

    c = GLA_CHUNK
    cm = cm_ref[...]
    ri = lax.broadcasted_iota(jnp.int32, (c, c), 0)
    ci = lax.broadcasted_iota(jnp.int32, (c, c), 1)
    nt = (((1,), (1,)), ((), ()))
    tn = (((0,), (0,)), ((), ()))
    units = [(hh, slice(ch * c, (ch + 1) * c), slice(hh * GLA_DK, (hh + 1) * GLA_DK),
              slice(hh * GLA_DV, (hh + 1) * GLA_DV)) for hh in range(GLA_HEADS) for ch in range(tm // c)]
    n_units = len(units)
    level_masks = []
    for n in GLA_LEVELS:
        sh = n.bit_length() - 1
        level_masks.append(((ri >> sh) == (ci >> sh) + 1) & (((ri >> sh) & 1) == 1))

    args, o_intra, q_in = [None] * n_units, [None] * n_units, [None] * n_units
    decays, upds, states = [None] * n_units, [None] * n_units, [None] * GLA_HEADS

    def hi_lo(rows, kl):
        g = la_s[rows, kl]
        ghi = g.astype(BF16)
        glo = (g - ghi.astype(F32)).astype(BF16)
        return jnp.concatenate([ghi, glo], axis=0)

    def exponents(u):
        (_, r0, kl0, _), (_, r1, kl1, _) = units[u], units[u + 1]
        pair = jnp.dot(cm, jnp.concatenate([hi_lo(r0, kl0), hi_lo(r1, kl1)], axis=1),
                       preferred_element_type=F32)
        for i in range(2):
            part = pair[:, i * GLA_DK:(i + 1) * GLA_DK]
            cum = part[0:c]
            blocks = [cum, cum[c - 1:c, :] - cum]
            for n in GLA_VECTOR_LEVELS:
                for b0 in range(0, c, 2 * n):
                    pivot = cum[b0 + n - 1:b0 + n, :]
                    blocks += [pivot - cum[b0:b0 + n, :], cum[b0 + n:b0 + 2 * n, :] - pivot]
            args[u + i] = jnp.concatenate(blocks + [part[c:]], axis=0)

    def within_chunk(u):
        _, rows, kl, vl = units[u]
        arg = args[u]
        ex = jnp.exp(arg.astype(BF16))
        qb = gq_s[rows, kl]
        kb = gk_s[rows, kl]
        vb = gv_s[rows, vl]
        sc = jnp.where(ri == ci, lax.dot_general(qb, kb, nt, preferred_element_type=F32), 0.0)
        for li in range(len(GLA_LEVELS)):
            ex_l = ex[(li + 2) * c:(li + 3) * c]
            s_l = lax.dot_general(qb * ex_l, kb * ex_l, nt, preferred_element_type=F32)
            sc = jnp.where(level_masks[li], s_l, sc)
        o_intra[u] = jnp.dot(sc.astype(BF16), vb, preferred_element_type=F32)
        q_in[u] = qb * ex[0:c]
        upds[u] = lax.dot_general(kb * ex[c:2 * c], vb, tn, preferred_element_type=F32)
        last = arg[c - 1:c, :]
        decay = jnp.broadcast_to(jnp.exp(last), (GLA_DK, GLA_DK)).T
        decays[u] = jnp.concatenate([decay, decay], axis=1)

    def across_chunks(u):
        hh, rows, _, vl = units[u]
        if states[hh] is None:
            states[hh] = state_ref[hh]
        state = states[hh]
        o = o_intra[u] + jnp.dot(q_in[u], state.astype(BF16), preferred_element_type=F32)
        states[hh] = state * decays[u] + upds[u]
        ms_o = jnp.mean(o * o, axis=-1, keepdims=True)
        y = o * lax.rsqrt(ms_o + RMS_EPS) * ng_ref[...]
        og_ref[rows, vl] = (y * sr_s[rows, vl].astype(F32)).astype(BF16)
        if u + 1 == n_units or units[u + 1][0] != hh:
            state_ref[hh] = states[hh]

    pieces = ([functools.partial(exponents, u) for u in range(0, n_units, 2)]
              + [functools.partial(within_chunk, u) for u in range(n_units)]
              + [functools.partial(across_chunks, u) for u in range(n_units)])
    n_rest = len(jobs) - len(feed)
    per_job = -(-len(pieces) // n_rest)

    lr = jnp.dot(h, w_ref[:, rank0:rank0 + LANES], preferred_element_type=F32)
    issued = [proj(j[2], j[3]) for j in jobs[:INPROJ_AHEAD]]
    for n, (ref, lo, _, cw, post) in enumerate(jobs):
        y = issued.pop(0)
        if n + INPROJ_AHEAD < len(jobs):
            issued.append(proj(jobs[n + INPROJ_AHEAD][2], jobs[n + INPROJ_AHEAD][3]))
        if n == 0:
            logit = jnp.dot(lr.astype(BF16), w2_ref[...], preferred_element_type=F32) + gb_ref[...]
            la_s[...] = _log_sigmoid(logit) * (1.0 / GLA_GATE_TAU)
        ref[:, lo:lo + cw] = post(y).astype(BF16)
        if n >= len(feed):
            for piece in pieces[:per_job]:
                piece()
            del pieces[:per_job]
    assert not pieces


def _in_projection_gla(x2, seq_len, g1, w, w2p, gb, qg, kg, avg, cm, ng):
    t = x2.shape[0]
    tm = INPROJ_ROW_TILE
    row = pl.BlockSpec((tm, D_MODEL), lambda i: (i, 0))
    bf = jax.ShapeDtypeStruct((t, D_MODEL), BF16)
    consts = (g1, w, w2p, gb, qg, kg, avg, cm, ng)
    return pl.pallas_call(
        functools.partial(_inproj_gla_kernel, steps_per_seq=seq_len // tm),
        grid=(t // tm,),
        in_specs=[row] + [_const_spec(c.shape) for c in consts],
        out_specs=(row,) * 6,
        out_shape=(bf,) * 6,
        scratch_shapes=[pltpu.VMEM((D_MODEL, 2 * D_MODEL), BF16),
                        pltpu.VMEM((tm, GLA_KEY_WIDTH), BF16), pltpu.VMEM((tm, GLA_KEY_WIDTH), BF16),
                        pltpu.VMEM((tm, GLA_VALUE_WIDTH), BF16), pltpu.VMEM((tm, GLA_VALUE_WIDTH), BF16),
                        pltpu.VMEM((tm, GLA_KEY_WIDTH), F32),
                        pltpu.VMEM((GLA_HEADS, GLA_DK, GLA_DV), F32)],
        compiler_params=pltpu.CompilerParams(dimension_semantics=("arbitrary",),
                                             vmem_limit_bytes=VMEM_LIMIT_BYTES),
        name="in_projection_gla",
    )(x2, *consts)


def _sb_kernel(q_ref, k_ref, v_ref, mt_ref, o_ref, acc_ref, carry_ref):
    qi = pl.program_id(2)
    tq, tk, qt = q_ref.shape[1], SB_TK, SB_QT
    sub = tq // tk
    nt = (((1,), (1,)), ((), ()))
    tn = (((0,), (0,)), ((), ()))
    mt = mt_ref[...]
    one = jnp.ones((), BF16)
    zero16 = jnp.zeros((), BF16)
    first = qi * sub

    def head_split(q):
        lane = lax.broadcasted_iota(jnp.int32, q.shape, 1)
        zero = jnp.zeros_like(q)
        return (jnp.where(lane < SB_HEAD_DIM, q, zero), jnp.where(lane >= SB_HEAD_DIM, q, zero))

    def key_block(ref, j):
        return ref[0, pl.ds(pl.multiple_of(j * tk, tk), tk), :]

    def softplus2(zt):
        l2 = jnp.log2((one + jnp.exp2(-jnp.abs(zt))).astype(F32)).astype(BF16)
        return jnp.maximum(zt, zero16) + l2

    def causal(rows, width):
        return (lax.broadcasted_iota(jnp.int32, (rows, width), 0)
                < lax.broadcasted_iota(jnp.int32, (rows, width), 1))

    top_mask = causal(qt, tk)
    br_mask = causal(qt, qt)
    dead = jnp.zeros((qt, qt), BF16)

    def with_dead_quadrant(top, br):
        return jnp.concatenate([top, jnp.concatenate([dead, br], axis=1)], axis=0)

    def diag_logits(g):
        kb = key_block(k_ref, first + g)
        qh = head_split(q_ref[0, g * tk:(g + 1) * tk, :])
        return [lax.dot_general(kb, qh[h], nt, preferred_element_type=F32).astype(BF16)
                for h in range(2)]

    def diag_log_not(zts):
        out = []
        for zt in zts:
            p_top = jnp.where(top_mask, softplus2(zt[:qt, :]), zero16)
            p_br = jnp.where(br_mask, softplus2(zt[qt:, qt:]), zero16)
            cs = jnp.dot(mt, with_dead_quadrant(p_top, p_br), preferred_element_type=F32)
            out.append((p_top, p_br, cs))
        return out

    def diag_weights(g, zts, pcs):
        vb = key_block(v_ref, first + g)
        out = []
        for h in range(2):
            p_top, p_br, cs = pcs[h]
            zt = zts[h]
            w_top = jnp.exp2((zt[:qt, :] - p_top) + cs[:qt, :].astype(BF16))
            w_br = jnp.exp2((zt[qt:, qt:] - p_br) + cs[qt:, qt:].astype(BF16))
            w = with_dead_quadrant(jnp.where(top_mask, w_top, zero16), jnp.where(br_mask, w_br, zero16))
            pv = lax.dot_general(vb, w, tn, preferred_element_type=F32)
            out.append((pv, cs[0:1, :] - p_top[0:1, :].astype(F32)))
        return out

    mt_half = mt[:qt, :qt]

    def half_block(ref, j):
        return ref[0, pl.ds(pl.multiple_of(j * qt, qt), qt), :]

    def tile_logits(j, qh):
        kb = half_block(k_ref, j)
        return lax.dot_general(kb, jnp.concatenate(qh, axis=0), nt,
                               preferred_element_type=F32).astype(BF16)

    def tile_log_not(zt):
        p = softplus2(zt)
        return p, jnp.dot(mt_half, p, preferred_element_type=F32)

    def tile_weights(j, zt, pc, rows):
        vb = half_block(v_ref, j)
        p, cs = pc
        w = jnp.exp2(((zt - p) + cs.astype(BF16)) + jnp.concatenate(rows, axis=1).astype(BF16))
        pv = lax.dot_general(vb, w, tn, preferred_element_type=F32)
        tot = cs[0:1, :] - p[0:1, :].astype(F32)
        return [(pv[:, h * qt:(h + 1) * qt], tot[:, h * qt:(h + 1) * qt]) for h in range(2)]

    no_prev_bias = jnp.where(first > 0, 0.0, SB_NO_BLOCK_BIAS).astype(F32)
    halves_per_block = tk // qt
    prev_block = [jnp.maximum((first + g) * halves_per_block - 1, 0) for g in range(sub)]
    n_units = 2 * sub
    zs, pcs, diag_out = {}, {}, {}

    def issue_logits(n):
        if n < sub:
            zs[n] = diag_logits(n)
        else:
            g = n - sub
            zs[n] = tile_logits(prev_block[g], head_split(q_ref[0, g * tk:g * tk + qt, :]))

    def issue_weights(n):
        if n < sub:
            diag_out[n] = diag_weights(n, zs.pop(n), pcs.pop(n))
            return
        g = n - sub
        rows = [diag_out[g][h][1][:, :qt] for h in range(2)]
        if g == 0:
            rows = [r + no_prev_bias for r in rows]
        res = tile_weights(prev_block[g], zs.pop(n), pcs.pop(n), rows)
        for h in range(2):
            pv_d, tot_d = diag_out[g][h]
            pv_p, tot_p = res[h]
            acc_ref[h, 2 * g] = pv_d[:, :qt] + pv_p
            acc_ref[h, 2 * g + 1] = pv_d[:, qt:]
            carry_ref[h, 2 * g:2 * g + 1, :] = tot_d[:, :qt] + tot_p
            carry_ref[h, 2 * g + 1:2 * g + 2, :] = tot_d[:, qt:]

    for n in range(SB_LOGITS_AHEAD):
        issue_logits(n)
    for n in range(n_units + SB_WEIGHTS_BEHIND):
        if n + SB_LOGITS_AHEAD < n_units:
            issue_logits(n + SB_LOGITS_AHEAD)
        if n < n_units:
            pcs[n] = diag_log_not(zs[n]) if n < sub else tile_log_not(zs[n])
        if n >= SB_WEIGHTS_BEHIND:
            issue_weights(n - SB_WEIGHTS_BEHIND)

    n_tiles = tq // qt
    tile_ids = lax.broadcasted_iota(jnp.int32, (n_tiles, 1), 0)

    def first_unvisited(c):
        return (first + (c >> 1)) * halves_per_block - 2 + (c & 1)

    next_block = first_unvisited(tile_ids)

    top = jnp.max(jnp.maximum(carry_ref[0], carry_ref[1]), axis=1, keepdims=True)
    todo = jnp.where((top >= SB_UNDERFLOW_LOG2) & (next_block >= 0), tile_ids, n_tiles)

    def next_live_tile(after):
        return jnp.min(jnp.where(todo > after, todo, n_tiles))

    def walk_tile(c):
        qh = head_split(q_ref[0, pl.ds(pl.multiple_of(c * qt, qt), qt), :])

        def live(state):
            j, top = state
            return jnp.logical_and(j >= 0, top >= SB_UNDERFLOW_LOG2)

        def step(state):
            j, _ = state
            zts = tile_logits(j, qh)
            rows = [carry_ref[h, pl.ds(c, 1), :] for h in range(2)]
            res = tile_weights(j, zts, tile_log_not(zts), rows)
            new_rows = [rows[h] + res[h][1] for h in range(2)]
            for h in range(2):
                acc_ref[h, c] += res[h][0]
                carry_ref[h, pl.ds(c, 1), :] = new_rows[h]
            return j - 1, jnp.max(jnp.maximum(new_rows[0], new_rows[1]))

        lax.while_loop(live, step, (first_unvisited(c), jnp.float32(0.0)))
        return next_live_tile(c)

    lax.while_loop(lambda c: c < n_tiles, walk_tile, next_live_tile(-1))

    row = lax.broadcasted_iota(jnp.int32, (LANES, qt), 0)
    for c in range(tq // qt):
        ot = jnp.where(row < SB_HEAD_DIM, acc_ref[0, c], acc_ref[1, c])
        o_ref[0, c * qt:(c + 1) * qt, :] = ot.T.astype(BF16)


def _sb_attention(q, k, v, mt):
    b, s, _ = q.shape
    tq = min(SB_TQ, s)
    assert s % tq == 0 and tq % SB_TK == 0 and SB_TK == 2 * SB_QT and SB_QT == LANES
    qspec = pl.BlockSpec((1, tq, LANES), lambda bi, hp, i: (bi, i, hp))
    kvspec = pl.BlockSpec((1, s, LANES), lambda bi, hp, i: (bi, 0, hp))
    return pl.pallas_call(
        _sb_kernel,
        grid=(b, SB_WIDTH // LANES, s // tq),
        in_specs=[qspec, kvspec, kvspec, _const_spec(mt.shape)],
        out_specs=qspec,
        out_shape=jax.ShapeDtypeStruct((b, s, SB_WIDTH), BF16),
        scratch_shapes=[pltpu.VMEM((2, tq // SB_QT, LANES, SB_QT), F32),
                        pltpu.VMEM((2, tq // SB_QT, SB_QT), F32)],
        compiler_params=pltpu.CompilerParams(
            dimension_semantics=("arbitrary", "arbitrary", "arbitrary"),
            vmem_limit_bytes=VMEM_LIMIT_BYTES),
        name="stick_breaking_attention",
    )(q, k, v, mt)


def _gla_pivot_rows(n):
    rows = np.arange(GLA_CHUNK)
    base = (rows // n) * n
    second = (rows // n) % 2 == 1
    return second, np.where(second, base - 1, base + n - 1)


def _gla_level_matrix():
    c = GLA_CHUNK
    r = np.arange(c)[:, None]
    m = np.arange(c)[None, :]
    mats = [m <= r]
    for n in GLA_MATMUL_LEVELS:
        _, idx = _gla_pivot_rows(n)
        lo = np.minimum(r[:, 0], idx)[:, None]
        hi = np.maximum(r[:, 0], idx)[:, None]
        mats.append((m > lo) & (m <= hi))
    one = np.concatenate(mats, axis=0).astype(np.float32)
    return np.concatenate([one, one], axis=1)


def _tail_kernel(x_ref, osb_ref, ogla_ref, gs_ref, gg_ref, wsb_ref, wgla_ref, wout_ref,
                 g2_ref, wg_ref, wu_ref, wd_ref, o_ref):
    tm = x_ref.shape[0]
    halves = [slice(0, tm // 2), slice(tm // 2, tm)]
    dot = functools.partial(jnp.dot, preferred_element_type=F32)
    ab = [(dot(osb_ref[sl, :], wsb_ref[...]), dot(ogla_ref[sl, :], wgla_ref[...])) for sl in halves]
    merged = [(gs_ref[sl, :].astype(F32) * a + gg_ref[sl, :].astype(F32) * b).astype(BF16)
              for sl, (a, b) in zip(halves, ab)]
    x1 = [x_ref[sl, :] + dot(m, wout_ref[...]) for sl, m in zip(halves, merged)]
    h2 = []
    for x1h in x1:
        ms = jnp.mean(x1h * x1h, axis=-1, keepdims=True)
        h2.append((x1h * lax.rsqrt(ms + RMS_EPS) * g2_ref[...]).astype(BF16))
    gu = [(dot(h, wg_ref[...]), dot(h, wu_ref[...])) for h in h2]
    hid = [(gate * jax.nn.sigmoid(gate) * up).astype(BF16) for gate, up in gu]
    for sl, x1h, hd in zip(halves, x1, hid):
        o_ref[sl, :] = x1h + dot(hd, wd_ref[...])


def _tail(x2, osb, ogla, gs, gg, wsb, wgla, wout, g2, wg, wu, wd):
    t = x2.shape[0]
    tm = ROW_TILE
    row = pl.BlockSpec((tm, D_MODEL), lambda i: (i, 0))
    consts = (wsb, wgla, wout, g2, wg, wu, wd)
    return pl.pallas_call(
        _tail_kernel,
        grid=(t // tm,),
        in_specs=[row] * 5 + [_const_spec(c.shape) for c in consts],
        out_specs=row,
        out_shape=jax.ShapeDtypeStruct((t, D_MODEL), F32),
        compiler_params=pltpu.CompilerParams(dimension_semantics=("arbitrary",),
                                             vmem_limit_bytes=VMEM_LIMIT_BYTES),
        name="merge_out_ffn",
    )(x2, osb, ogla, gs, gg, *consts)


def _layer(x, norm1_g, w_in, sb_q_norm_g, sb_k_norm_g, gla_gate_w2, gla_gate_b, gla_out_norm_g,
           w_branch_sb, w_branch_gla, w_out, norm2_g, w_ffn_gate, w_ffn_up, w_ffn_down):
    b, s, d = x.shape
    t = b * s
    x2 = x.reshape(t, d)

    w2p = jnp.pad(gla_gate_w2, ((0, LANES - GLA_GATE_RANK), (0, 0))).astype(BF16)

    heads_per_tile = MXU_DIM // SB_HEAD_DIM
    qg = jnp.tile(sb_q_norm_g * (SB_HEAD_DIM ** -0.5 * LOG2E), heads_per_tile).reshape(1, MXU_DIM)
    kg = jnp.tile(sb_k_norm_g, heads_per_tile).reshape(1, MXU_DIM)
    hid = np.arange(MXU_DIM) // SB_HEAD_DIM
    avg = jnp.asarray((hid[:, None] == hid[None, :]).astype(np.float32) / SB_HEAD_DIM, BF16)

    cm = jnp.asarray(_gla_level_matrix(), BF16)
    q, k, v, gs, gg, o_gla = _in_projection_gla(
        x2, s, norm1_g.reshape(1, d), w_in.astype(BF16), w2p, gla_gate_b.reshape(1, -1), qg, kg, avg,
        cm, gla_out_norm_g.reshape(1, -1))

    idx = np.arange(SB_TK)
    mt = jnp.asarray(-(idx[None, :] > idx[:, None]).astype(np.float32), BF16)
    o_sb = _sb_attention(q.reshape(b, s, -1), k.reshape(b, s, -1), v.reshape(b, s, -1), mt)

    out = _tail(x2, o_sb.reshape(t, -1), o_gla, gs, gg,
                w_branch_sb.astype(BF16), w_branch_gla.astype(BF16), w_out.astype(BF16),
                norm2_g.reshape(1, d), w_ffn_gate.astype(BF16), w_ffn_up.astype(BF16),
                w_ffn_down.astype(BF16))
    return out.reshape(b, s, d)


def kernel(x, norm1_g, w_in, sb_q_norm_g, sb_k_norm_g, gla_gate_w2, gla_gate_b, gla_out_norm_g,
           w_branch_sb, w_branch_gla, w_out, norm2_g, w_ffn_gate, w_ffn_up, w_ffn_down):
    for l in range(norm1_g.shape[0]):
        x = _layer(x, norm1_g[l], w_in[l], sb_q_norm_g[l], sb_k_norm_g[l], gla_gate_w2[l],
                   gla_gate_b[l], gla_out_norm_g[l], w_branch_sb[l], w_branch_gla[l], w_out[l],
                   norm2_g[l], w_ffn_gate[l], w_ffn_up[l], w_ffn_down[l])
    return x
```

```python
import functools

import jax
import jax.numpy as jnp
import numpy as np
from jax import lax
from jax.experimental import pallas as pl
from jax.experimental.pallas import tpu as pltpu

F32 = jnp.float32
BF16 = jnp.bfloat16

D_MODEL = 1024
SB_HEADS = 16
SB_HEAD_DIM = 64
SB_WIDTH = SB_HEADS * SB_HEAD_DIM
GLA_HEADS = 4
GLA_KEY_WIDTH = D_MODEL // 2
GLA_VALUE_WIDTH = D_MODEL
GLA_DK = GLA_KEY_WIDTH // GLA_HEADS
GLA_DV = GLA_VALUE_WIDTH // GLA_HEADS
GLA_GATE_RANK = 16
GLA_GATE_TAU = 16.0
RMS_EPS = 1e-6
LOG2E = 1.4426950408889634

LANES = 128
MXU_DIM = 256
VMEM_LIMIT_BYTES = 56 * 1024 * 1024

ROW_TILE = 512
INPROJ_ROW_TILE = 512
INPROJ_CHUNK = 512
INPROJ_AHEAD = 2
SB_TQ = 4096
SB_TK = 256
SB_QT = 128
SB_UNDERFLOW_LOG2 = -126.0
SB_NO_BLOCK_BIAS = -1e30
SB_LOGITS_AHEAD = 3
SB_WEIGHTS_BEHIND = 2
GLA_CHUNK = 64
GLA_VECTOR_LEVELS = (32, 16, 8)
GLA_MATMUL_LEVELS = (4, 2, 1)
GLA_LEVELS = GLA_VECTOR_LEVELS + GLA_MATMUL_LEVELS
assert GLA_DV == 2 * GLA_DK


def _const_spec(shape):
    nd = len(shape)
    return pl.BlockSpec(shape, lambda *_: (0,) * nd, pipeline_mode=pl.Buffered(1))


def _log_sigmoid(x):
    return jnp.minimum(x, 0.0) - jnp.log(1.0 + jnp.exp(-jnp.abs(x)))


def _inproj_gla_kernel(x_ref, g1_ref, w_ref, w2_ref, gb_ref, qg_ref, kg_ref, avg_ref,
                       cm_ref, ng_ref,
                       q_ref, k_ref, v_ref, gs_ref, gg_ref, og_ref,
                       wb_s, gq_s, gk_s, gv_s, sr_s, la_s, state_ref, *, steps_per_seq):
    tm = x_ref.shape[0]

    @pl.when(pl.program_id(0) % steps_per_seq == 0)
    def _():
        state_ref[...] = jnp.zeros_like(state_ref)

    x = x_ref[...]
    ms = jnp.mean(x * x, axis=-1, keepdims=True)
    h = (x * lax.rsqrt(ms + RMS_EPS) * g1_ref[...]).astype(BF16)
    rank0 = 3 * SB_WIDTH + 2 * GLA_KEY_WIDTH + 2 * GLA_VALUE_WIDTH
    rank1 = rank0 + GLA_GATE_RANK

    @pl.when(pl.program_id(0) == 0)
    def _():
        wb_s[...] = w_ref[:, rank1:]

    def proj(c0, cw):
        w = w_ref[:, c0:c0 + cw] if c0 < rank0 else wb_s[:, c0 - rank0:c0 - rank0 + cw]
        return jnp.dot(h, w, preferred_element_type=F32)

    def head_norm(y, gain):
        m = jnp.dot((y * y).astype(BF16), avg_ref[...], preferred_element_type=F32)
        return y * lax.rsqrt(m + RMS_EPS) * gain

    segments = (
        (q_ref, SB_WIDTH, MXU_DIM, lambda y: head_norm(y, qg_ref[...])),
        (k_ref, SB_WIDTH, MXU_DIM, lambda y: head_norm(y, kg_ref[...])),
        (v_ref, SB_WIDTH, INPROJ_CHUNK, lambda y: y),
        (gq_s, GLA_KEY_WIDTH, INPROJ_CHUNK, lambda y: y * (GLA_DK ** -0.5)),
        (gk_s, GLA_KEY_WIDTH, INPROJ_CHUNK, lambda y: y),
        (gv_s, GLA_VALUE_WIDTH, INPROJ_CHUNK, lambda y: y),
        (sr_s, GLA_VALUE_WIDTH, INPROJ_CHUNK, lambda y: y * jax.nn.sigmoid(y)),
        (gs_ref, D_MODEL, INPROJ_CHUNK, jax.nn.sigmoid),
        (gg_ref, D_MODEL, INPROJ_CHUNK, jax.nn.sigmoid),
    )
    feed, normed, plain, off = [], [], [], 0
    for ref, width, cw, post in segments:
        cw = min(cw, width)
        group = (normed if (ref is q_ref or ref is k_ref)
                 else feed if (ref is gq_s or ref is gk_s or ref is gv_s or ref is sr_s) else plain)
        group.extend((ref, lo, off + lo, cw, post) for lo in range(0, width, cw))
        off += width
    jobs = list(feed)
    while normed or plain:
        if plain:
            jobs.append(plain.pop(0))
        if normed:
            jobs.append(normed.pop(0))

    c = GLA_CHUNK
    cm = cm_ref[...]
    ri = lax.broadcasted_iota(jnp.int32, (c, c), 0)
    ci = lax.broadcasted_iota(jnp.int32, (c, c), 1)
    nt = (((1,), (1,)), ((), ()))
    tn = (((0,), (0,)), ((), ()))
    units = [(hh, slice(ch * c, (ch + 1) * c), slice(hh * GLA_DK, (hh + 1) * GLA_DK),
              slice(hh * GLA_DV, (hh + 1) * GLA_DV)) for hh in range(GLA_HEADS) for ch in range(tm // c)]
    n_units = len(units)
    level_masks = []
    for n in GLA_LEVELS:
        sh = n.bit_length() - 1
        level_masks.append(((ri >> sh) == (ci >> sh) + 1) & (((ri >> sh) & 1) == 1))

    args, o_intra, q_in = [None] * n_units, [None] * n_units, [None] * n_units
    decays, upds, states = [None] * n_units, [None] * n_units, [None] * GLA_HEADS

    def hi_lo(rows, kl):
        g = la_s[rows, kl]
        ghi = g.astype(BF16)
        glo = (g - ghi.astype(F32)).astype(BF16)
        return jnp.concatenate([ghi, glo], axis=0)

    def exponents(u):
        (_, r0, kl0, _), (_, r1, kl1, _) = units[u], units[u + 1]
        pair = jnp.dot(cm, jnp.concatenate([hi_lo(r0, kl0), hi_lo(r1, kl1)], axis=1),
                       preferred_element_type=F32)
        for i in range(2):
            part = pair[:, i * GLA_DK:(i + 1) * GLA_DK]
            cum = part[0:c]
            blocks = [cum, cum[c - 1:c, :] - cum]
            for n in GLA_VECTOR_LEVELS:
                for b0 in range(0, c, 2 * n):
                    pivot = cum[b0 + n - 1:b0 + n, :]
                    blocks += [pivot - cum[b0:b0 + n, :], cum[b0 + n:b0 + 2 * n, :] - pivot]
            args[u + i] = jnp.concatenate(blocks + [part[c:]], axis=0)

    def within_chunk(u):
        _, rows, kl, vl = units[u]
        arg = args[u]
        ex = jnp.exp(arg.astype(BF16))
        qb = gq_s[rows, kl]
        kb = gk_s[rows, kl]
        vb = gv_s[rows, vl]
        sc = jnp.where(ri == ci, lax.dot_general(qb, kb, nt, preferred_element_type=F32), 0.0)
        for li in range(len(GLA_LEVELS)):
            ex_l = ex[(li + 2) * c:(li + 3) * c]
            s_l = lax.dot_general(qb * ex_l, kb * ex_l, nt, preferred_element_type=F32)
            sc = jnp.where(level_masks[li], s_l, sc)
        o_intra[u] = jnp.dot(sc.astype(BF16), vb, preferred_element_type=F32)
        q_in[u] = qb * ex[0:c]
        upds[u] = lax.dot_general(kb * ex[c:2 * c], vb, tn, preferred_element_type=F32)
        last = arg[c - 1:c, :]
        decay = jnp.broadcast_to(jnp.exp(last), (GLA_DK, GLA_DK)).T
        decays[u] = jnp.concatenate([decay, decay], axis=1)

    def across_chunks(u):
        hh, rows, _, vl = units[u]
        if states[hh] is None:
            states[hh] = state_ref[hh]
        state = states[hh]
        o = o_intra[u] + jnp.dot(q_in[u], state.astype(BF16), preferred_element_type=F32)
        states[hh] = state * decays[u] + upds[u]
        ms_o = jnp.mean(o * o, axis=-1, keepdims=True)
        y = o * lax.rsqrt(ms_o + RMS_EPS) * ng_ref[...]
        og_ref[rows, vl] = (y * sr_s[rows, vl].astype(F32)).astype(BF16)
        if u + 1 == n_units or units[u + 1][0] != hh:
            state_ref[hh] = states[hh]

    pieces = ([functools.partial(exponents, u) for u in range(0, n_units, 2)]
              + [functools.partial(within_chunk, u) for u in range(n_units)]
              + [functools.partial(across_chunks, u) for u in range(n_units)])
    n_rest = len(jobs) - len(feed)
    per_job = -(-len(pieces) // n_rest)

    lr = jnp.dot(h, w_ref[:, rank0:rank0 + LANES], preferred_element_type=F32)
    issued = [proj(j[2], j[3]) for j in jobs[:INPROJ_AHEAD]]
    for n, (ref, lo, _, cw, post) in enumerate(jobs):
        y = issued.pop(0)
        if n + INPROJ_AHEAD < len(jobs):
            issued.append(proj(jobs[n + INPROJ_AHEAD][2], jobs[n + INPROJ_AHEAD][3]))
        if n == 0:
            logit = jnp.dot(lr.astype(BF16), w2_ref[...], preferred_element_type=F32) + gb_ref[...]
            la_s[...] = _log_sigmoid(logit) * (1.0 / GLA_GATE_TAU)
        ref[:, lo:lo + cw] = post(y).astype(BF16)
        if n >= len(feed):
            for piece in pieces[:per_job]:
                piece()
            del pieces[:per_job]
    assert not pieces


def _in_projection_gla(x2, seq_len, g1, w, w2p, gb, qg, kg, avg, cm, ng):
    t = x2.shape[0]
    tm = INPROJ_ROW_TILE
    row = pl.BlockSpec((tm, D_MODEL), lambda i: (i, 0))
    bf = jax.ShapeDtypeStruct((t, D_MODEL), BF16)
    consts = (g1, w, w2p, gb, qg, kg, avg, cm, ng)
    return pl.pallas_call(
        functools.partial(_inproj_gla_kernel, steps_per_seq=seq_len // tm),
        grid=(t // tm,),
        in_specs=[row] + [_const_spec(c.shape) for c in consts],
        out_specs=(row,) * 6,
        out_shape=(bf,) * 6,
        scratch_shapes=[pltpu.VMEM((D_MODEL, 2 * D_MODEL), BF16),
                        pltpu.VMEM((tm, GLA_KEY_WIDTH), BF16), pltpu.VMEM((tm, GLA_KEY_WIDTH), BF16),
                        pltpu.VMEM((tm, GLA_VALUE_WIDTH), BF16), pltpu.VMEM((tm, GLA_VALUE_WIDTH), BF16),
                        pltpu.VMEM((tm, GLA_KEY_WIDTH), F32),
                        pltpu.VMEM((GLA_HEADS, GLA_DK, GLA_DV), F32)],
        compiler_params=pltpu.CompilerParams(dimension_semantics=("arbitrary",),
                                             vmem_limit_bytes=VMEM_LIMIT_BYTES),
        name="in_projection_gla",
    )(x2, *consts)


def _sb_kernel(q_ref, k_ref, v_ref, mt_ref, o_ref, acc_ref, carry_ref):
    qi = pl.program_id(2)
    tq, tk, qt = q_ref.shape[1], SB_TK, SB_QT
    sub = tq // tk
    nt = (((1,), (1,)), ((), ()))
    tn = (((0,), (0,)), ((), ()))
    mt = mt_ref[...]
    one = jnp.ones((), BF16)
    zero16 = jnp.zeros((), BF16)
    first = qi * sub

    def head_split(q):
        lane = lax.broadcasted_iota(jnp.int32, q.shape, 1)
        zero = jnp.zeros_like(q)
        return (jnp.where(lane < SB_HEAD_DIM, q, zero), jnp.where(lane >= SB_HEAD_DIM, q, zero))

    def key_block(ref, j):
        return ref[0, pl.ds(pl.multiple_of(j * tk, tk), tk), :]

    def softplus2(zt):
        l2 = jnp.log2((one + jnp.exp2(-jnp.abs(zt))).astype(F32)).astype(BF16)
        return jnp.maximum(zt, zero16) + l2

    def causal(rows, width):
        return (lax.broadcasted_iota(jnp.int32, (rows, width), 0)
                < lax.broadcasted_iota(jnp.int32, (rows, width), 1))

    top_mask = causal(qt, tk)
    br_mask = causal(qt, qt)
    dead = jnp.zeros((qt, qt), BF16)

    def with_dead_quadrant(top, br):
        return jnp.concatenate([top, jnp.concatenate([dead, br], axis=1)], axis=0)

    def diag_logits(g):
        kb = key_block(k_ref, first + g)
        qh = head_split(q_ref[0, g * tk:(g + 1) * tk, :])
        return [lax.dot_general(kb, qh[h], nt, preferred_element_type=F32).astype(BF16)
                for h in range(2)]

    def diag_log_not(zts):
        out = []
        for zt in zts:
            p_top = jnp.where(top_mask, softplus2(zt[:qt, :]), zero16)
            p_br = jnp.where(br_mask, softplus2(zt[qt:, qt:]), zero16)
            cs = jnp.dot(mt, with_dead_quadrant(p_top, p_br), preferred_element_type=F32)
            out.append((p_top, p_br, cs))
        return out

    def diag_weights(g, zts, pcs):
        vb = key_block(v_ref, first + g)
        out = []
        for h in range(2):
            p_top, p_br, cs = pcs[h]
            zt = zts[h]
            w_top = jnp.exp2((zt[:qt, :] - p_top) + cs[:qt, :].astype(BF16))
            w_br = jnp.exp2((zt[qt:, qt:] - p_br) + cs[qt:, qt:].astype(BF16))
            w = with_dead_quadrant(jnp.where(top_mask, w_top, zero16), jnp.where(br_mask, w_br, zero16))
            pv = lax.dot_general(vb, w, tn, preferred_element_type=F32)
            out.append((pv, cs[0:1, :] - p_top[0:1, :].astype(F32)))
        return out

    mt_half = mt[:qt, :qt]

    def half_block(ref, j):
        return ref[0, pl.ds(pl.multiple_of(j * qt, qt), qt), :]

    def tile_logits(j, qh):
        kb = half_block(k_ref, j)
        return lax.dot_general(kb, jnp.concatenate(qh, axis=0), nt,
                               preferred_element_type=F32).astype(BF16)

    def tile_log_not(zt):
        p = softplus2(zt)
        return p, jnp.dot(mt_half, p, preferred_element_type=F32)

    def tile_weights(j, zt, pc, rows):
        vb = half_block(v_ref, j)
        p, cs = pc
        w = jnp.exp2(((zt - p) + cs.astype(BF16)) + jnp.concatenate(rows, axis=1).astype(BF16))
        pv = lax.dot_general(vb, w, tn, preferred_element_type=F32)
        tot = cs[0:1, :] - p[0:1, :].astype(F32)
        return [(pv[:, h * qt:(h + 1) * qt], tot[:, h * qt:(h + 1) * qt]) for h in range(2)]

    no_prev_bias = jnp.where(first > 0, 0.0, SB_NO_BLOCK_BIAS).astype(F32)
    halves_per_block = tk // qt
    prev_block = [jnp.maximum((first + g) * halves_per_block - 1, 0) for g in range(sub)]
    n_units = 2 * sub
    zs, pcs, diag_out = {}, {}, {}

    def issue_logits(n):
        if n < sub:
            zs[n] = diag_logits(n)
        else:
            g = n - sub
            zs[n] = tile_logits(prev_block[g], head_split(q_ref[0, g * tk:g * tk + qt, :]))

    def issue_weights(n):
        if n < sub:
            diag_out[n] = diag_weights(n, zs.pop(n), pcs.pop(n))
            return
        g = n - sub
        rows = [diag_out[g][h][1][:, :qt] for h in range(2)]
        if g == 0:
            rows = [r + no_prev_bias for r in rows]
        res = tile_weights(prev_block[g], zs.pop(n), pcs.pop(n), rows)
        for h in range(2):
            pv_d, tot_d = diag_out[g][h]
            pv_p, tot_p = res[h]
            acc_ref[h, 2 * g] = pv_d[:, :qt] + pv_p
            acc_ref[h, 2 * g + 1] = pv_d[:, qt:]
            carry_ref[h, 2 * g:2 * g + 1, :] = tot_d[:, :qt] + tot_p
            carry_ref[h, 2 * g + 1:2 * g + 2, :] = tot_d[:, qt:]

    for n in range(SB_LOGITS_AHEAD):
        issue_logits(n)
    for n in range(n_units + SB_WEIGHTS_BEHIND):
        if n + SB_LOGITS_AHEAD < n_units:
            issue_logits(n + SB_LOGITS_AHEAD)
        if n < n_units:
            pcs[n] = diag_log_not(zs[n]) if n < sub else tile_log_not(zs[n])
        if n >= SB_WEIGHTS_BEHIND:
            issue_weights(n - SB_WEIGHTS_BEHIND)

    n_tiles = tq // qt
    tile_ids = lax.broadcasted_iota(jnp.int32, (n_tiles, 1), 0)

    def first_unvisited(c):
        return (first + (c >> 1)) * halves_per_block - 2 + (c & 1)

    next_block = first_unvisited(tile_ids)

    top = jnp.max(jnp.maximum(carry_ref[0], carry_ref[1]), axis=1, keepdims=True)
    todo = jnp.where((top >= SB_UNDERFLOW_LOG2) & (next_block >= 0), tile_ids, n_tiles)

    def next_live_tile(after):
        return jnp.min(jnp.where(todo > after, todo, n_tiles))

    def walk_pair(c1):
        c2 = next_live_tile(c1)
        has2 = c2 < n_tiles
        other = jnp.where(c1 + 1 < n_tiles, c1 + 1, 0)
        tiles = (c1, jnp.where(has2, c2, other))
        qhs = [head_split(q_ref[0, pl.ds(pl.multiple_of(c * qt, qt), qt), :]) for c in tiles]

        def running(j, top):
            return jnp.logical_and(j >= 0, top >= SB_UNDERFLOW_LOG2)

        def live(state):
            return jnp.logical_or(running(state[0], state[1]), running(state[2], state[3]))

        def step(state):
            js, tops = (state[0], state[2]), (state[1], state[3])
            on = [running(js[i], tops[i]) for i in range(2)]
            blocks = [jnp.maximum(js[i], 0) for i in range(2)]
            rows = [[carry_ref[h, pl.ds(tiles[i], 1), :] for h in range(2)] for i in range(2)]
            zts = [tile_logits(blocks[i], qhs[i]) for i in range(2)]
            pcs = [tile_log_not(zts[i]) for i in range(2)]
            res = []
            for i in range(2):
                gate = jnp.where(on[i], 0.0, SB_NO_BLOCK_BIAS).astype(F32)
                res.append(tile_weights(blocks[i], zts[i], pcs[i], [r + gate for r in rows[i]]))
            out = []
            for i in range(2):
                keep = jnp.where(on[i], 1.0, 0.0).astype(F32)
                new_rows = [rows[i][h] + keep * res[i][h][1] for h in range(2)]
                for h in range(2):
                    acc_ref[h, tiles[i]] += res[i][h][0]
                    carry_ref[h, pl.ds(tiles[i], 1), :] = new_rows[h]
                top = jnp.max(jnp.maximum(new_rows[0], new_rows[1]))
                out += [jnp.where(on[i], js[i] - 1, js[i]), jnp.where(on[i], top, tops[i])]
            return tuple(out)

        start2 = jnp.where(has2, first_unvisited(c2), -1)
        lax.while_loop(live, step, (first_unvisited(c1), jnp.float32(0.0), start2, jnp.float32(0.0)))
        return next_live_tile(jnp.where(has2, c2, c1))

    lax.while_loop(lambda c: c < n_tiles, walk_pair, next_live_tile(-1))

    row = lax.broadcasted_iota(jnp.int32, (LANES, qt), 0)
    for c in range(tq // qt):
        ot = jnp.where(row < SB_HEAD_DIM, acc_ref[0, c], acc_ref[1, c])
        o_ref[0, c * qt:(c + 1) * qt, :] = ot.T.astype(BF16)


def _sb_attention(q, k, v, mt):
    b, s, _ = q.shape
    tq = min(SB_TQ, s)
    assert s % tq == 0 and tq % SB_TK == 0 and SB_TK == 2 * SB_QT and SB_QT == LANES
    qspec = pl.BlockSpec((1, tq, LANES), lambda bi, hp, i: (bi, i, hp))
    kvspec = pl.BlockSpec((1, s, LANES), lambda bi, hp, i: (bi, 0, hp))
    return pl.pallas_call(
        _sb_kernel,
        grid=(b, SB_WIDTH // LANES, s // tq),
        in_specs=[qspec, kvspec, kvspec, _const_spec(mt.shape)],
        out_specs=qspec,
        out_shape=jax.ShapeDtypeStruct((b, s, SB_WIDTH), BF16),
        scratch_shapes=[pltpu.VMEM((2, tq // SB_QT, LANES, SB_QT), F32),
                        pltpu.VMEM((2, tq // SB_QT, SB_QT), F32)],
        compiler_params=pltpu.CompilerParams(
            dimension_semantics=("arbitrary", "arbitrary", "arbitrary"),
            vmem_limit_bytes=VMEM_LIMIT_BYTES),
        name="stick_breaking_attention",
    )(q, k, v, mt)


def _gla_pivot_rows(n):
    rows = np.arange(GLA_CHUNK)
    base = (rows // n) * n
    second = (rows // n) % 2 == 1
    return second, np.where(second, base - 1, base + n - 1)


def _gla_level_matrix():
    c = GLA_CHUNK
    r = np.arange(c)[:, None]
    m = np.arange(c)[None, :]
    mats = [m <= r]
    for n in GLA_MATMUL_LEVELS:
        _, idx = _gla_pivot_rows(n)
        lo = np.minimum(r[:, 0], idx)[:, None]
        hi = np.maximum(r[:, 0], idx)[:, None]
        mats.append((m > lo) & (m <= hi))
    one = np.concatenate(mats, axis=0).astype(np.float32)
    return np.concatenate([one, one], axis=1)


def _tail_kernel(x_ref, osb_ref, ogla_ref, gs_ref, gg_ref, wsb_ref, wgla_ref, wout_ref,
                 g2_ref, wg_ref, wu_ref, wd_ref, o_ref):
    tm = x_ref.shape[0]
    halves = [slice(0, tm // 2), slice(tm // 2, tm)]
    dot = functools.partial(jnp.dot, preferred_element_type=F32)
    ab = [(dot(osb_ref[sl, :], wsb_ref[...]), dot(ogla_ref[sl, :], wgla_ref[...])) for sl in halves]
    merged = [(gs_ref[sl, :].astype(F32) * a + gg_ref[sl, :].astype(F32) * b).astype(BF16)
              for sl, (a, b) in zip(halves, ab)]
    x1 = [x_ref[sl, :] + dot(m, wout_ref[...]) for sl, m in zip(halves, merged)]
    h2 = []
    for x1h in x1:
        ms = jnp.mean(x1h * x1h, axis=-1, keepdims=True)
        h2.append((x1h * lax.rsqrt(ms + RMS_EPS) * g2_ref[...]).astype(BF16))
    gu = [(dot(h, wg_ref[...]), dot(h, wu_ref[...])) for h in h2]
    hid = [(gate * jax.nn.sigmoid(gate) * up).astype(BF16) for gate, up in gu]
    for sl, x1h, hd in zip(halves, x1, hid):
        o_ref[sl, :] = x1h + dot(hd, wd_ref[...])


def _tail(x2, osb, ogla, gs, gg, wsb, wgla, wout, g2, wg, wu, wd):
    t = x2.shape[0]
    tm = ROW_TILE
    row = pl.BlockSpec((tm, D_MODEL), lambda i: (i, 0))
    consts = (wsb, wgla, wout, g2, wg, wu, wd)
    return pl.pallas_call(
        _tail_kernel,
        grid=(t // tm,),
        in_specs=[row] * 5 + [_const_spec(c.shape) for c in consts],
        out_specs=row,
        out_shape=jax.ShapeDtypeStruct((t, D_MODEL), F32),
        compiler_params=pltpu.CompilerParams(dimension_semantics=("arbitrary",),
                                             vmem_limit_bytes=VMEM_LIMIT_BYTES),
        name="merge_out_ffn",
    )(x2, osb, ogla, gs, gg, *consts)


def _layer(x, norm1_g, w_in, sb_q_norm_g, sb_k_norm_g, gla_gate_w2, gla_gate_b, gla_out_norm_g,
           w_branch_sb, w_branch_gla, w_out, norm2_g, w_ffn_gate, w_ffn_up, w_ffn_down):
    b, s, d = x.shape
    t = b * s
    x2 = x.reshape(t, d)

    w2p = jnp.pad(gla_gate_w2, ((0, LANES - GLA_GATE_RANK), (0, 0))).astype(BF16)

    heads_per_tile = MXU_DIM // SB_HEAD_DIM
    qg = jnp.tile(sb_q_norm_g * (SB_HEAD_DIM ** -0.5 * LOG2E), heads_per_tile).reshape(1, MXU_DIM)
    kg = jnp.tile(sb_k_norm_g, heads_per_tile).reshape(1, MXU_DIM)
    hid = np.arange(MXU_DIM) // SB_HEAD_DIM
    avg = jnp.asarray((hid[:, None] == hid[None, :]).astype(np.float32) / SB_HEAD_DIM, BF16)

    cm = jnp.asarray(_gla_level_matrix(), BF16)
    q, k, v, gs, gg, o_gla = _in_projection_gla(
        x2, s, norm1_g.reshape(1, d), w_in.astype(BF16), w2p, gla_gate_b.reshape(1, -1), qg, kg, avg,
        cm, gla_out_norm_g.reshape(1, -1))

    idx = np.arange(SB_TK)
    mt = jnp.asarray(-(idx[None, :] > idx[:, None]).astype(np.float32), BF16)
    o_sb = _sb_attention(q.reshape(b, s, -1), k.reshape(b, s, -1), v.reshape(b, s, -1), mt)

    out = _tail(x2, o_sb.reshape(t, -1), o_gla, gs, gg,
                w_branch_sb.astype(BF16), w_branch_gla.astype(BF16), w_out.astype(BF16),
                norm2_g.reshape(1, d), w_ffn_gate.astype(BF16), w_ffn_up.astype(BF16),
                w_ffn_down.astype(BF16))
    return out.reshape(b, s, d)


def kernel(x, norm1_g, w_in, sb_q_norm_g, sb_k_norm_g, gla_gate_w2, gla_gate_b, gla_out_norm_g,
           w_branch_sb, w_branch_gla, w_out, norm2_g, w_ffn_gate, w_ffn_up, w_ffn_down):
    for l in range(norm1_g.shape[0]):
        x = _layer(x, norm1_g[l], w_in[l], sb_q_norm_g[l], sb_k_norm_g[l], gla_gate_w2[l],
                   gla_gate_b[l], gla_out_norm_g[l], w_branch_sb[l], w_branch_gla[l], w_out[l],
                   norm2_g[l], w_ffn_gate[l], w_ffn_up[l], w_ffn_down[l])
    return x
```
